```python
import math
import jax, jax.numpy as jnp
from jax import lax
import numpy as np

D_MODEL = 1024
BATCH = 8
SEQ = 2048
DEPTH = 2

N_MIXERS = 2
EPS = 1e-6
A_HEADS = 16
A_KV_HEADS = 4
A_HEAD_DIM = D_MODEL // A_HEADS
A_GROUP = A_HEADS // A_KV_HEADS
A_QKV_DIM = (A_HEADS + 2 * A_KV_HEADS) * A_HEAD_DIM
WINDOW = 128
ROPE_THETA = 10000.0
B_HEADS = 8
B_HEAD_DIM = D_MODEL // B_HEADS
B_KEY_DIM = B_HEADS * B_HEAD_DIM
B_IN_DIM = 4 * B_KEY_DIM + 2 * B_HEADS
CONV_WIDTH = 4
CHUNK = 64
D_FF = 3584
N_EXPERTS = 8
TOP_K = 2

kernel_name = "hybrid_swa_sink_gdn_moe"


def rms_norm(x, w):
    xf = x.astype(jnp.float32)
    y = xf * lax.rsqrt(jnp.mean(xf * xf, axis=-1, keepdims=True) + EPS)
    return (y * w.astype(jnp.float32)).astype(x.dtype)


def rope_tables(seq):
    inv = 1.0 / (ROPE_THETA ** (jnp.arange(0, A_HEAD_DIM, 2, dtype=jnp.float32) / A_HEAD_DIM))
    ang = jnp.arange(seq, dtype=jnp.float32)[:, None] * inv[None, :]
    return jnp.cos(ang), jnp.sin(ang)


def apply_rope(x, cos, sin):
    x1, x2 = jnp.split(x, 2, axis=-1)
    c = cos[None, :, None, :].astype(x.dtype)
    s = sin[None, :, None, :].astype(x.dtype)
    return jnp.concatenate([x1 * c - x2 * s, x2 * c + x1 * s], axis=-1)


def swa_sink_attention(h, w_qkv, b_qkv, sinks, w_o, cos, sin):
    B, T, _ = h.shape
    qkv = h @ w_qkv + b_qkv
    q, k, v = jnp.split(qkv, [A_HEADS * A_HEAD_DIM, (A_HEADS + A_KV_HEADS) * A_HEAD_DIM], axis=-1)
    q = apply_rope(q.reshape(B, T, A_HEADS, A_HEAD_DIM), cos, sin)
    k = apply_rope(k.reshape(B, T, A_KV_HEADS, A_HEAD_DIM), cos, sin)
    v = v.reshape(B, T, A_KV_HEADS, A_HEAD_DIM)
    nb = T // WINDOW
    qb = q.reshape(B, nb, WINDOW, A_KV_HEADS, A_GROUP, A_HEAD_DIM)

    def band(t):
        tb = t.reshape(B, nb, WINDOW, A_KV_HEADS, A_HEAD_DIM)
        prev = jnp.pad(tb[:, :-1], ((0, 0), (1, 0), (0, 0), (0, 0), (0, 0)))
        return jnp.concatenate([prev, tb], axis=2)

    kb, vb = band(k), band(v)
    scores = jnp.einsum('bnqhgd,bnkhd->bnhgqk', qb, kb).astype(jnp.float32) * (A_HEAD_DIM ** -0.5)
    qi = jnp.arange(WINDOW)[:, None]
    kj = jnp.arange(2 * WINDOW)[None, :]
    rel = WINDOW + qi - kj
    in_band = (rel >= 0) & (rel < WINDOW)
    blk = jnp.arange(nb)[:, None, None]
    valid = in_band[None] & ((blk > 0) | (kj[None] >= WINDOW))
    scores = jnp.where(valid[None, :, None, None], scores, -jnp.inf)
    sink = jnp.broadcast_to(sinks.astype(jnp.float32).reshape(1, 1, A_KV_HEADS, A_GROUP, 1, 1),
                            scores.shape[:-1] + (1,))
    probs = jax.nn.softmax(jnp.concatenate([scores, sink], axis=-1), axis=-1)[..., :-1]
    o = jnp.einsum('bnhgqk,bnkhd->bnqhgd', probs.astype(vb.dtype), vb)
    return o.reshape(B, T, A_HEADS * A_HEAD_DIM) @ w_o


def causal_depthwise_conv(x, w):
    C = x.shape[-1]
    return lax.conv_general_dilated(x, w[:, None, :].astype(x.dtype), window_strides=(1,),
                                    padding=[(CONV_WIDTH - 1, 0)],
                                    dimension_numbers=('NWC', 'WIO', 'NWC'),
                                    feature_group_count=C)


def l2_normalize(x):
    return x * lax.rsqrt(jnp.sum(x * x, axis=-1, keepdims=True) + EPS)


def chunked_gated_delta_rule(q, k, v, g, beta):
    B, T, H, D = q.shape
    n = T // CHUNK

    def chunks(t):
        return jnp.moveaxis(t.reshape((B, n, CHUNK) + t.shape[2:]), 3, 2)

    qc = chunks(q * (D ** -0.5))
    kc, vc = chunks(k), chunks(v)
    gc = jnp.cumsum(chunks(g), axis=-1)
    bc = chunks(beta)
    idx = jnp.arange(CHUNK)
    causal = idx[:, None] >= idx[None, :]
    strict = idx[:, None] > idx[None, :]
    decay = jnp.exp(jnp.where(causal, gc[..., :, None] - gc[..., None, :], -jnp.inf))
    kb = kc * bc[..., None]
    lower = jnp.where(strict, jnp.einsum('bnhid,bnhjd->bnhij', kb, kc) * decay, 0.0)
    eye = jnp.eye(CHUNK, dtype=jnp.float32)
    rhs = jnp.concatenate([vc * bc[..., None], kb * jnp.exp(gc)[..., None]], axis=-1)
    sol = lax.linalg.triangular_solve(eye + lower, rhs, left_side=True, lower=True)
    u, w = jnp.split(sol, 2, axis=-1)
    intra = jnp.einsum('bnhid,bnhjd->bnhij', qc, kc) * decay

    def step(S, inp):
        q_i, k_i, u_i, w_i, g_i, a_i = inp
        v_new = u_i - jnp.einsum('bhcd,bhde->bhce', w_i, S)
        o = (jnp.einsum('bhcd,bhde->bhce', q_i * jnp.exp(g_i)[..., None], S)
             + jnp.einsum('bhij,bhje->bhie', a_i, v_new))
        g_last = g_i[..., -1:]
        S = (S * jnp.exp(g_last)[..., None]
             + jnp.einsum('bhcd,bhce->bhde', k_i * jnp.exp(g_last - g_i)[..., None], v_new))
        return S, o

    xs = tuple(jnp.moveaxis(t, 1, 0) for t in (qc, kc, u, w, gc, intra))
    S0 = jnp.zeros((B, H, D, D), jnp.float32)
    _, o = lax.scan(step, S0, xs)
    o = jnp.swapaxes(jnp.moveaxis(o, 0, 1), 2, 3)
    return o.reshape(B, T, H, D)


def gated_deltanet(h, w_in, conv_w, a_log, dt_bias, o_norm, w_o):
    B, T, _ = h.shape
    proj = h @ w_in
    qkv, z, b, a = jnp.split(proj, [3 * B_KEY_DIM, 4 * B_KEY_DIM, 4 * B_KEY_DIM + B_HEADS], axis=-1)
    qkv = jax.nn.silu(causal_depthwise_conv(qkv, conv_w))
    q, k, v = [t.reshape(B, T, B_HEADS, B_HEAD_DIM).astype(jnp.float32) for t in jnp.split(qkv, 3, axis=-1)]
    q, k = l2_normalize(q), l2_normalize(k)
    beta = jax.nn.sigmoid(b.astype(jnp.float32))
    g = -jnp.exp(a_log.astype(jnp.float32)) * jax.nn.softplus(a.astype(jnp.float32) + dt_bias.astype(jnp.float32))
    o = chunked_gated_delta_rule(q, k, v, g, beta)
    o = rms_norm(o, o_norm) * jax.nn.silu(z.reshape(B, T, B_HEADS, B_HEAD_DIM).astype(jnp.float32))
    return o.reshape(B, T, B_KEY_DIM).astype(h.dtype) @ w_o


def swiglu(h, w_gate, w_up, w_down):
    return (jax.nn.silu(h @ w_gate) * (h @ w_up)) @ w_down


def moe_swiglu(h, w_router, w_gate, w_up, w_down):
    B, T, D = h.shape
    ht = h.reshape(B * T, D)
    logits = (ht @ w_router).astype(jnp.float32)
    top_logits, top_idx = lax.top_k(logits, TOP_K)
    top_w = jax.nn.softmax(top_logits, axis=-1)
    combine = jnp.sum(jax.nn.one_hot(top_idx, N_EXPERTS, dtype=jnp.float32) * top_w[..., None], axis=1)
    out = jnp.zeros_like(ht)
    for e in range(N_EXPERTS):
        y = swiglu(ht, w_gate[e], w_up[e], w_down[e])
        out = out + combine[:, e:e + 1].astype(ht.dtype) * y
    return out.reshape(B, T, D)


def setup_inputs(seed: int = 0) -> dict:
    key = jax.random.key(seed)
    ks = iter(jax.random.split(key, 32))
    n_a = (DEPTH + 1) // 2
    n_b = DEPTH // 2

    def normal(shape, scale):
        return jax.random.normal(next(ks), shape, jnp.float32) * scale

    def gain(shape):
        return 1.0 + 0.02 * jax.random.normal(next(ks), shape, jnp.float32)

    x = jax.random.normal(next(ks), (BATCH, SEQ, D_MODEL), jnp.float32)
    dt = jnp.exp(jax.random.uniform(next(ks), (n_b, B_HEADS), jnp.float32, math.log(1e-3), math.log(1e-1)))
    dt_bias = dt + jnp.log(-jnp.expm1(-dt))
    a_log = jnp.log(jax.random.uniform(next(ks), (n_b, B_HEADS), jnp.float32, 1.0, 16.0))
    return {
        'x': x,
        'a_norm': gain((n_a, D_MODEL)),
        'a_w_qkv': normal((n_a, D_MODEL, A_QKV_DIM), D_MODEL ** -0.5),
        'a_b_qkv': normal((n_a, A_QKV_DIM), 0.02),
        'a_sinks': normal((n_a, A_HEADS), 1.0),
        'a_w_o': normal((n_a, A_HEADS * A_HEAD_DIM, D_MODEL), (A_HEADS * A_HEAD_DIM) ** -0.5),
        'f_norm': gain((n_a, D_MODEL)),
        'f_w_gate': normal((n_a, D_MODEL, D_FF), D_MODEL ** -0.5),
        'f_w_up': normal((n_a, D_MODEL, D_FF), D_MODEL ** -0.5),
        'f_w_down': normal((n_a, D_FF, D_MODEL), D_FF ** -0.5),
        'b_norm': gain((n_b, D_MODEL)),
        'b_w_in': normal((n_b, D_MODEL, B_IN_DIM), D_MODEL ** -0.5),
        'b_conv_w': normal((n_b, CONV_WIDTH, 3 * B_KEY_DIM), CONV_WIDTH ** -0.5),
        'b_a_log': a_log,
        'b_dt_bias': dt_bias,
        'b_o_norm': gain((n_b, B_HEAD_DIM)),
        'b_w_o': normal((n_b, B_KEY_DIM, D_MODEL), B_KEY_DIM ** -0.5),
        'm_norm': gain((n_b, D_MODEL)),
        'm_w_router': normal((n_b, D_MODEL, N_EXPERTS), D_MODEL ** -0.5),
        'm_w_gate': normal((n_b, N_EXPERTS, D_MODEL, D_FF), D_MODEL ** -0.5),
        'm_w_up': normal((n_b, N_EXPERTS, D_MODEL, D_FF), D_MODEL ** -0.5),
        'm_w_down': normal((n_b, N_EXPERTS, D_FF, D_MODEL), D_FF ** -0.5),
        'final_norm': gain((D_MODEL,)),
    }


def reference(x, a_norm, a_w_qkv, a_b_qkv, a_sinks, a_w_o, f_norm, f_w_gate, f_w_up, f_w_down,
              b_norm, b_w_in, b_conv_w, b_a_log, b_dt_bias, b_o_norm, b_w_o,
              m_norm, m_w_router, m_w_gate, m_w_up, m_w_down, final_norm):
    cos, sin = rope_tables(x.shape[1])
    h = x
    for layer in range(DEPTH):
        j = layer // N_MIXERS
        if layer % N_MIXERS == 0:
            h = h + swa_sink_attention(rms_norm(h, a_norm[j]), a_w_qkv[j], a_b_qkv[j], a_sinks[j], a_w_o[j], cos, sin)
            h = h + swiglu(rms_norm(h, f_norm[j]), f_w_gate[j], f_w_up[j], f_w_down[j])
        else:
            h = h + gated_deltanet(rms_norm(h, b_norm[j]), b_w_in[j], b_conv_w[j], b_a_log[j], b_dt_bias[j], b_o_norm[j], b_w_o[j])
            h = h + moe_swiglu(rms_norm(h, m_norm[j]), m_w_router[j], m_w_gate[j], m_w_up[j], m_w_down[j])
    return rms_norm(h, final_norm)
```

```python
import functools
import math

import jax
import jax.numpy as jnp
from jax import lax
from jax.experimental import pallas as pl
from jax.experimental.pallas import tpu as pltpu

F32 = jnp.float32
BF16 = jnp.bfloat16
I32 = jnp.int32

EPS = 1e-6
LANES = 128
SUBLANES = 8
VMEM_LIMIT = 56 * 1024 * 1024

A_HEADS, A_KV_HEADS, A_HEAD_DIM = 16, 4, 64
WINDOW = 128
ROPE_THETA = 10000.0
B_HEADS, B_HEAD_DIM = 8, 128
CONV_WIDTH = 4
N_EXPERTS = 8
GDN_CHUNK = 128
NEG = -1e30

ROW_TILE = 512
FF_TILE = 512
COMBINE_TILE = 256


def _cparams(sem):
    return pltpu.CompilerParams(dimension_semantics=sem, vmem_limit_bytes=VMEM_LIMIT)


def _rms(x, g):
    var = jnp.mean(x * x, axis=-1, keepdims=True)
    return x * lax.rsqrt(var + EPS) * g


def _silu(x):
    return x * (1.0 / (1.0 + jnp.exp(-x)))


def _qkv_kernel(x_ref, g_ref, w_ref, b_ref, cq_ref, sq_ref, ck_ref, sk_ref, q_ref, k_ref, v_ref):
    tm = x_ref.shape[0]
    xn = _rms(x_ref[...], g_ref[...]).astype(BF16)
    lane = lax.broadcasted_iota(I32, (tm, LANES), 1)
    first_half = (lane % A_HEAD_DIM) < (A_HEAD_DIM // 2)

    def rope(xs, c, s):
        sw = jnp.where(first_half, pltpu.roll(xs, LANES - 32, 1), pltpu.roll(xs, 32, 1))
        return xs * c + sw * s

    nq = q_ref.shape[1] // LANES
    nk = k_ref.shape[1] // LANES
    nv = v_ref.shape[1] // LANES
    for s in range(0, nq + nk + nv, 2):
        acc = jnp.dot(xn, w_ref[:, s * LANES:(s + 2) * LANES], preferred_element_type=F32)
        acc = acc + b_ref[:, s * LANES:(s + 2) * LANES]
        for t in range(2):
            col = s + t
            part = acc[:, t * LANES:(t + 1) * LANES]
            if col < nq:
                q_ref[:, col * LANES:(col + 1) * LANES] = rope(part, cq_ref[...], sq_ref[...]).astype(BF16)
            elif col < nq + nk:
                c = col - nq
                k_ref[:, c * LANES:(c + 1) * LANES] = rope(part, ck_ref[...], sk_ref[...]).astype(BF16)
            else:
                c = col - nq - nk
                v_ref[:, c * LANES:(c + 1) * LANES] = part.astype(BF16)


def _qkv_proj(x2, g, w, b, cq, sq, ck, sk, seq):
    n, d = x2.shape
    tm = ROW_TILE
    nq = A_HEADS * A_HEAD_DIM
    nkv = 2 * A_KV_HEADS * A_HEAD_DIM
    tblk = seq // tm
    row = lambda i: (i, 0)
    const = lambda i: (0, 0)
    tab = lambda i: (i % tblk, 0)
    return pl.pallas_call(
        _qkv_kernel,
        grid=(n // tm,),
        in_specs=[
            pl.BlockSpec((tm, d), row),
            pl.BlockSpec((1, d), const),
            pl.BlockSpec((d, nq + 2 * nkv), const),
            pl.BlockSpec((1, nq + 2 * nkv), const),
            pl.BlockSpec((tm, LANES), tab), pl.BlockSpec((tm, LANES), tab),
            pl.BlockSpec((tm, LANES), tab), pl.BlockSpec((tm, LANES), tab),
        ],
        out_specs=[pl.BlockSpec((tm, nq), row), pl.BlockSpec((tm, nkv), row), pl.BlockSpec((tm, nkv), row)],
        out_shape=[jax.ShapeDtypeStruct((n, nq), BF16), jax.ShapeDtypeStruct((n, nkv), BF16),
                   jax.ShapeDtypeStruct((n, nkv), BF16)],
        compiler_params=_cparams(("parallel",)),
        name="qkv_rope",
    )(x2, g, w, b, cq, sq, ck, sk)


def _attn_kernel(sink_ref, q_ref, kp_ref, kc_ref, vp_ref, vc_ref, o_ref):
    n = pl.program_id(1)
    w = WINDOW
    group = A_HEADS // A_KV_HEADS
    rows = group * w
    lo = lax.broadcasted_iota(I32, (w, LANES), 1) < A_HEAD_DIM
    qi = lax.broadcasted_iota(I32, (rows, 2 * w), 0) % w
    kj = lax.broadcasted_iota(I32, (rows, 2 * w), 1)
    valid = (kj > qi) & (kj <= qi + w) & ((n > 0) | (kj >= w))
    hrow = lax.broadcasted_iota(I32, (rows, 1), 0) // w
    zero = jnp.zeros((w, LANES), BF16)
    for g in range(A_KV_HEADS):
        cs = slice(g * LANES, (g + 1) * LANES)
        kd = jnp.concatenate([kp_ref[:, cs], kc_ref[:, cs]], axis=0)
        vd = jnp.concatenate([vp_ref[:, cs], vc_ref[:, cs]], axis=0)
        parts = []
        for p in range(group // 2):
            j = g * (group // 2) + p
            q2 = q_ref[:, j * LANES:(j + 1) * LANES]
            parts.append(jnp.where(lo, q2, zero))
            parts.append(jnp.where(lo, zero, q2))
        qs = jnp.concatenate(parts, axis=0)
        s = lax.dot_general(qs, kd, (((1,), (1,)), ((), ())), preferred_element_type=F32)
        s = jnp.where(valid, s, NEG)
        sk = jnp.full((rows, 1), sink_ref[g * group], F32)
        for h in range(1, group):
            sk = jnp.where(hrow == h, sink_ref[g * group + h], sk)
        m = jnp.maximum(jnp.max(s, axis=-1, keepdims=True), sk)
        p_ = jnp.exp(s - m)
        den = jnp.sum(p_, axis=-1, keepdims=True) + jnp.exp(sk - m)
        o = jnp.dot(p_.astype(BF16), vd, preferred_element_type=F32) / den
        for p in range(group // 2):
            j = g * (group // 2) + p
            o2 = jnp.where(lo, o[2 * p * w:(2 * p + 1) * w], o[(2 * p + 1) * w:(2 * p + 2) * w])
            o_ref[:, j * LANES:(j + 1) * LANES] = o2.astype(BF16)


def _attention(q, k, v, sinks, batch, seq):
    n, dq = q.shape
    dkv = k.shape[1]
    nb = seq // WINDOW
    cur = lambda b, i: (b * nb + i, 0)
    prev = lambda b, i: (b * nb + jnp.maximum(i - 1, 0), 0)
    return pl.pallas_call(
        _attn_kernel,
        grid=(batch, nb),
        in_specs=[
            pl.BlockSpec(memory_space=pltpu.SMEM),
            pl.BlockSpec((WINDOW, dq), cur),
            pl.BlockSpec((WINDOW, dkv), prev), pl.BlockSpec((WINDOW, dkv), cur),
            pl.BlockSpec((WINDOW, dkv), prev), pl.BlockSpec((WINDOW, dkv), cur),
        ],
        out_specs=pl.BlockSpec((WINDOW, dq), cur),
        out_shape=jax.ShapeDtypeStruct((n, dq), BF16),
        compiler_params=_cparams(("parallel", "parallel")),
        name="swa_attention",
    )(sinks, q, k, k, v, v)


def _proj_kernel(a_ref, w_ref, res_ref, g_ref, h_ref, hn_ref, *, tiled_rows):
    h = res_ref[...] + jnp.dot(a_ref[...], w_ref[...], preferred_element_type=F32)
    h_ref[...] = h
    hn = _rms(h, g_ref[...])
    if tiled_rows:
        tm = h.shape[0]
        for s in range(h.shape[1] // LANES):
            hn_ref[pl.ds(s, tm, stride=SUBLANES), :] = hn[:, s * LANES:(s + 1) * LANES]
    else:
        hn_ref[...] = hn.astype(hn_ref.dtype)


def _proj_res_norm(a, w, res, g, *, tiled_rows):
    n, kdim = a.shape
    d = w.shape[1]
    tm = ROW_TILE
    row = lambda i: (i, 0)
    const = lambda i: (0, 0)
    if tiled_rows:
        assert d == SUBLANES * LANES
        hn_spec = pl.BlockSpec((tm * SUBLANES, LANES), row)
        hn_shape = jax.ShapeDtypeStruct((n * SUBLANES, LANES), F32)
    else:
        hn_spec = pl.BlockSpec((tm, d), row)
        hn_shape = jax.ShapeDtypeStruct((n, d), BF16)
    return pl.pallas_call(
        functools.partial(_proj_kernel, tiled_rows=tiled_rows),
        grid=(n // tm,),
        in_specs=[pl.BlockSpec((tm, kdim), row), pl.BlockSpec((kdim, d), const),
                  pl.BlockSpec((tm, d), row), pl.BlockSpec((1, d), const)],
        out_specs=[pl.BlockSpec((tm, d), row), hn_spec],
        out_shape=[jax.ShapeDtypeStruct((n, d), F32), hn_shape],
        compiler_params=_cparams(("parallel",)),
        name="proj_res_norm",
    )(a, w, res, g)


def _swiglu_partial(xb, wg, wu, wd):
    hg = jnp.dot(xb, wg, preferred_element_type=F32)
    hu = jnp.dot(xb, wu, preferred_element_type=F32)
    act = (_silu(hg) * hu).astype(BF16)
    return jnp.dot(act, wd, preferred_element_type=F32)


def _ffn_dense_kernel(x_ref, wg_ref, wu_ref, wd_ref, res_ref, g_ref, h_ref, hn_ref, acc_ref):
    j = pl.program_id(1)

    @pl.when(j == 0)
    def _():
        acc_ref[...] = res_ref[...]

    acc_ref[...] += _swiglu_partial(x_ref[...], wg_ref[...], wu_ref[...], wd_ref[...])

    @pl.when(j == pl.num_programs(1) - 1)
    def _():
        h = acc_ref[...]
        h_ref[...] = h
        hn_ref[...] = _rms(h, g_ref[...]).astype(hn_ref.dtype)


def _ffn_dense(xn, wg, wu, wd, res, g):
    n, d = xn.shape
    ff = wg.shape[1]
    tm, tf = ROW_TILE, FF_TILE
    row = lambda i, j: (i, 0)
    return pl.pallas_call(
        _ffn_dense_kernel,
        grid=(n // tm, ff // tf),
        in_specs=[
            pl.BlockSpec((tm, d), row),
            pl.BlockSpec((d, tf), lambda i, j: (0, j)),
            pl.BlockSpec((d, tf), lambda i, j: (0, j)),
            pl.BlockSpec((tf, d), lambda i, j: (j, 0)),
            pl.BlockSpec((tm, d), row),
            pl.BlockSpec((1, d), lambda i, j: (0, 0)),
        ],
        out_specs=[pl.BlockSpec((tm, d), row), pl.BlockSpec((tm, d), row)],
        out_shape=[jax.ShapeDtypeStruct((n, d), F32), jax.ShapeDtypeStruct((n, d), BF16)],
        scratch_shapes=[pltpu.VMEM((tm, d), F32)],
        compiler_params=_cparams(("parallel", "arbitrary")),
        name="swiglu_dense",
    )(xn, wg, wu, wd, res, g)


def _ffn_moe_kernel(vt_ref, ve_ref, vlo_ref, vhi_ref, vfirst_ref, nv_ref,
                    x_ref, wg_ref, wu_ref, wd_ref, y_ref, acc_ref, xb_ref):
    v = pl.program_id(0)
    j = pl.program_id(1)
    tm, d = acc_ref.shape
    nslab = d // LANES

    @pl.when(v < nv_ref[0])
    def _():
        @pl.when(j == 0)
        def _():
            slabs = [x_ref[pl.ds(s, tm, stride=SUBLANES), :] for s in range(nslab)]
            xb_ref[...] = jnp.concatenate(slabs, axis=-1).astype(BF16)
            acc_ref[...] = jnp.zeros_like(acc_ref)

        acc_ref[...] += _swiglu_partial(xb_ref[...], wg_ref[...], wu_ref[...], wd_ref[...])

        @pl.when(j == pl.num_programs(1) - 1)
        def _():
            r = vt_ref[v] * tm + lax.broadcasted_iota(I32, (tm, 1), 0)
            mine = (r >= vlo_ref[v]) & (r < vhi_ref[v])

            @pl.when(vfirst_ref[v] == 1)
            def _():
                for s in range(nslab):
                    new = acc_ref[:, s * LANES:(s + 1) * LANES]
                    y_ref[pl.ds(s, tm, stride=SUBLANES), :] = jnp.where(mine, new, 0.0)

            @pl.when(vfirst_ref[v] != 1)
            def _():
                for s in range(nslab):
                    rows = pl.ds(s, tm, stride=SUBLANES)
                    new = acc_ref[:, s * LANES:(s + 1) * LANES]
                    y_ref[rows, :] = jnp.where(mine, new, y_ref[rows, :])


def _ffn_moe(xs_t, wg, wu, wd, visits, n_visits_max):
    vt, ve, vlo, vhi, vfirst, nv = visits
    rows8, _ = xs_t.shape
    p = rows8 // SUBLANES
    d = wg.shape[1]
    ff = wg.shape[2]
    tm, tf = ROW_TILE, FF_TILE
    nf = ff // tf

    def jj(v, j, nv_ref):
        return jnp.where(v < nv_ref[0], j, nf - 1)

    xmap = lambda v, j, vt, ve, vlo, vhi, vf, nv: (vt[v], 0)
    gmap = lambda v, j, vt, ve, vlo, vhi, vf, nv: (ve[v], 0, jj(v, j, nv))
    dmap = lambda v, j, vt, ve, vlo, vhi, vf, nv: (ve[v], jj(v, j, nv), 0)
    grid_spec = pltpu.PrefetchScalarGridSpec(
        num_scalar_prefetch=6,
        grid=(n_visits_max, nf),
        in_specs=[
            pl.BlockSpec((tm * SUBLANES, LANES), xmap),
            pl.BlockSpec((None, d, tf), gmap),
            pl.BlockSpec((None, d, tf), gmap),
            pl.BlockSpec((None, tf, d), dmap),
        ],
        out_specs=pl.BlockSpec((tm * SUBLANES, LANES), xmap),
        scratch_shapes=[pltpu.VMEM((tm, d), F32), pltpu.VMEM((tm, d), BF16)],
    )
    return pl.pallas_call(
        _ffn_moe_kernel,
        grid_spec=grid_spec,
        out_shape=jax.ShapeDtypeStruct((p * SUBLANES, LANES), F32),
        compiler_params=_cparams(("arbitrary", "arbitrary")),
        name="swiglu_moe",
    )(vt, ve, vlo, vhi, vfirst, nv, xs_t, wg, wu, wd)


def _mm_kernel(x_ref, w_ref, o_ref):
    o_ref[...] = jnp.dot(x_ref[...], w_ref[...], preferred_element_type=F32).astype(o_ref.dtype)


def _matmul(x, w, tn, out_dtype):
    n, kdim = x.shape
    m = w.shape[1]
    tm = ROW_TILE
    return pl.pallas_call(
        _mm_kernel,
        grid=(m // tn, n // tm),
        in_specs=[pl.BlockSpec((tm, kdim), lambda c, r: (r, 0)), pl.BlockSpec((kdim, tn), lambda c, r: (0, c))],
        out_specs=pl.BlockSpec((tm, tn), lambda c, r: (r, c)),
        out_shape=jax.ShapeDtypeStruct((n, m), out_dtype),
        compiler_params=_cparams(("parallel", "parallel")),
        name="matmul",
    )(x, w)


def _shift_rows(x, s, row):
    return jnp.where(row >= s, pltpu.roll(x, s, 0), 0.0)


def _gdn_kernel(alog_ref, dtb_ref, q_ref, k_ref, v_ref, z_ref, gt_ref, cwq_ref, cwk_ref, cwv_ref, on_ref,
                o_ref, qs_ref, ks_ref, vs_ref, gc_ref, bt_ref, s_ref):
    h = pl.program_id(1)
    t, dh = q_ref.shape
    c = GDN_CHUNK
    row = lax.broadcasted_iota(I32, (t, dh), 0)

    def conv_silu(x_ref, w_ref):
        x = x_ref[...]
        w = w_ref[...]
        y = x * w[CONV_WIDTH - 1:CONV_WIDTH, :]
        for s in range(1, CONV_WIDTH):
            y = y + _shift_rows(x, s, row) * w[CONV_WIDTH - 1 - s:CONV_WIDTH - s, :]
        return _silu(y)

    def l2n(x):
        return x * lax.rsqrt(jnp.sum(x * x, axis=-1, keepdims=True) + EPS)

    qs_ref[...] = l2n(conv_silu(q_ref, cwq_ref)) * (dh ** -0.5)
    ks_ref[...] = l2n(conv_silu(k_ref, cwk_ref))
    vs_ref[...] = conv_silu(v_ref, cwv_ref)

    gt = gt_ref[...]
    lane = lax.broadcasted_iota(I32, gt.shape, 1)
    b_log = jnp.sum(jnp.where(lane == h, gt, 0.0), axis=-1, keepdims=True)
    a_log = jnp.sum(jnp.where(lane == h + B_HEADS, gt, 0.0), axis=-1, keepdims=True)
    bt_ref[...] = jnp.broadcast_to(1.0 / (1.0 + jnp.exp(-b_log)), (t, dh))
    xa = a_log + dtb_ref[h]
    softplus = jnp.maximum(xa, 0.0) + jnp.log(1.0 + jnp.exp(-jnp.abs(xa)))
    g = jnp.broadcast_to(-jnp.exp(alog_ref[h]) * softplus, (t, dh))
    pos = row % c
    sh = 1
    while sh < c:
        g = g + jnp.where(pos >= sh, pltpu.roll(g, sh, 0), 0.0)
        sh *= 2
    gc_ref[...] = g

    s_ref[...] = jnp.zeros_like(s_ref)
    ii = lax.broadcasted_iota(I32, (c, c), 0)
    jx = lax.broadcasted_iota(I32, (c, c), 1)
    causal = ii >= jx
    strict = ii > jx
    bits = ii ^ jx
    eye = jnp.where(ii == jx, 1.0, 0.0)
    nt = (((1,), (1,)), ((), ()))
    tn = (((0,), (0,)), ((), ()))

    def chunk(ci, carry):
        r0 = pl.multiple_of(ci * c, c)
        rs = pl.ds(r0, c)
        qc = qs_ref[rs, :]
        kc = ks_ref[rs, :]
        vc = vs_ref[rs, :]
        gcb = gc_ref[rs, :]
        beta = bt_ref[rs, :]
        diff = gcb - gcb.T
        decay = jnp.exp(jnp.where(causal, diff, NEG))
        kb = kc * beta
        kcb = kc.astype(BF16)
        both = lax.dot_general(jnp.concatenate([kb.astype(BF16), qc.astype(BF16)], axis=0), kcb, nt,
                               preferred_element_type=F32)
        lower = jnp.where(strict, both[:c] * decay, 0.0)
        intra = jnp.where(causal, both[c:] * decay, 0.0)
        eg = jnp.exp(gcb)
        tinv = eye - jnp.where(bits < 2, lower, 0.0)
        m = 2
        while m < c:
            off = jnp.where((bits >= m) & (bits < 2 * m), lower, 0.0).astype(BF16)
            tb = tinv.astype(BF16)
            y = jnp.dot(off, tb, preferred_element_type=F32).astype(BF16)
            tinv = tinv - jnp.dot(tb, y, preferred_element_type=F32)
            m *= 2
        rhs = jnp.concatenate([vc * beta, kb * eg], axis=-1)
        x = rhs + jnp.dot((tinv - eye).astype(BF16), rhs.astype(BF16), preferred_element_type=F32)
        u = x[:, :dh]
        w = x[:, dh:]
        glast = gcb[c - 1:c, :]
        state = s_ref[...]
        wq = jnp.concatenate([w.astype(BF16), (qc * eg).astype(BF16)], axis=0)
        xs = jnp.dot(wq, state.astype(BF16), preferred_element_type=F32)
        vnew = (u - xs[:c]).astype(BF16)
        o = xs[c:] + jnp.dot(intra.astype(BF16), vnew, preferred_element_type=F32)
        kdec = (kc * jnp.exp(glast - gcb)).astype(BF16)
        s_ref[...] = state * jnp.exp(glast) + lax.dot_general(kdec, vnew, tn, preferred_element_type=F32)
        on = _rms(o, on_ref[...]) * _silu(z_ref[rs, :])
        o_ref[rs, :] = on.astype(o_ref.dtype)
        return carry

    lax.fori_loop(0, t // c, chunk, 0)


def _gated_deltanet(qkvz, gates, conv_w, a_log, dt_bias, o_norm, batch, seq):
    n = qkvz.shape[0]
    dh = B_HEAD_DIM
    nh = B_HEADS
    blk = lambda off: pl.BlockSpec((seq, dh), lambda b, h, off=off: (b, off + h))
    cw = lambda off: pl.BlockSpec((CONV_WIDTH, dh), lambda b, h, off=off: (0, off + h))
    smem = pl.BlockSpec(memory_space=pltpu.SMEM)
    return pl.pallas_call(
        _gdn_kernel,
        grid=(batch, nh),
        in_specs=[smem, smem, blk(0), blk(nh), blk(2 * nh), blk(3 * nh),
                  pl.BlockSpec((seq, LANES), lambda b, h: (b, 0)),
                  cw(0), cw(nh), cw(2 * nh),
                  pl.BlockSpec((1, dh), lambda b, h: (0, 0))],
        out_specs=pl.BlockSpec((seq, dh), lambda b, h: (b, h)),
        out_shape=jax.ShapeDtypeStruct((n, nh * dh), BF16),
        scratch_shapes=[pltpu.VMEM((seq, dh), F32)] * 5 + [pltpu.VMEM((dh, dh), F32)],
        compiler_params=_cparams(("parallel", "parallel")),
        name="gated_deltanet",
    )(a_log, dt_bias, qkvz, qkvz, qkvz, qkvz, gates, conv_w, conv_w, conv_w, o_norm)


def _route_kernel(hn_ref, wr_ref, e1_ref, e2_ref, w1_ref, w2_ref, r1_ref, r2_ref, cnt_ref, base_ref):
    i = pl.program_id(0)
    tm, d = hn_ref.shape[0] // SUBLANES, SUBLANES * LANES
    ne = wr_ref.shape[0]

    @pl.when(i == 0)
    def _():
        base_ref[...] = jnp.zeros_like(base_ref)

    lg = jnp.zeros((ne, tm), F32)
    for s in range(SUBLANES):
        lg = lg + lax.dot_general(wr_ref[:, s * LANES:(s + 1) * LANES], hn_ref[pl.ds(s, tm, stride=SUBLANES), :],
                                  (((1,), (1,)), ((), ())), precision=lax.Precision.HIGHEST,
                                  preferred_element_type=F32)
    eio = lax.broadcasted_iota(I32, (ne, tm), 0)
    m1 = jnp.max(lg, axis=0, keepdims=True)
    i1 = jnp.min(jnp.where(lg == m1, eio, ne), axis=0, keepdims=True)
    lg2 = jnp.where(eio == i1, -jnp.inf, lg)
    m2 = jnp.max(lg2, axis=0, keepdims=True)
    i2 = jnp.min(jnp.where(lg2 == m2, eio, ne), axis=0, keepdims=True)
    ex = jnp.exp(m2 - m1)
    w1 = 1.0 / (1.0 + ex)
    sel = jnp.where((eio == i1) | (eio == i2), 1.0, 0.0)
    tri = jnp.where(lax.broadcasted_iota(I32, (tm, tm), 0) < lax.broadcasted_iota(I32, (tm, tm), 1), 1.0, 0.0)
    pre = jnp.dot(sel.astype(BF16), tri.astype(BF16), preferred_element_type=F32) + base_ref[:, 0:1]
    r1 = jnp.sum(jnp.where(eio == i1, pre, 0.0), axis=0, keepdims=True)
    r2 = jnp.sum(jnp.where(eio == i2, pre, 0.0), axis=0, keepdims=True)
    base_ref[...] += jnp.sum(sel, axis=1, keepdims=True)
    e1_ref[0] = i1
    e2_ref[0] = i2
    w1_ref[0] = w1
    w2_ref[0] = ex * w1
    r1_ref[0] = r1.astype(I32)
    r2_ref[0] = r2.astype(I32)
    cnt_ref[...] = base_ref[...]


def _route(hn_t, wr_t):
    n = hn_t.shape[0] // SUBLANES
    ne = wr_t.shape[0]
    tm = ROW_TILE
    nt = n // tm
    vec = pl.BlockSpec((1, 1, tm), lambda i: (i, 0, 0))
    ish = jax.ShapeDtypeStruct((nt, 1, tm), I32)
    fsh = jax.ShapeDtypeStruct((nt, 1, tm), F32)
    return pl.pallas_call(
        _route_kernel,
        grid=(nt,),
        in_specs=[pl.BlockSpec((tm * SUBLANES, LANES), lambda i: (i, 0)),
                  pl.BlockSpec((ne, SUBLANES * LANES), lambda i: (0, 0))],
        out_specs=[vec, vec, vec, vec, vec, vec, pl.BlockSpec((ne, LANES), lambda i: (0, 0))],
        out_shape=[ish, ish, fsh, fsh, ish, ish, jax.ShapeDtypeStruct((ne, LANES), F32)],
        scratch_shapes=[pltpu.VMEM((ne, LANES), F32)],
        compiler_params=_cparams(("arbitrary",)),
        name="moe_route",
    )(hn_t, wr_t)


def _dispatch_kernel(p1_ref, p2_ref, hn_ref, xs_ref, sem):
    i = pl.program_id(0)
    tg = ROW_TILE

    def tile_copy(t, pos):
        src = hn_ref.at[pl.ds(pl.multiple_of(t * SUBLANES, SUBLANES), SUBLANES), :]
        dst = xs_ref.at[pl.ds(pl.multiple_of(pos * SUBLANES, SUBLANES), SUBLANES), :]
        return pltpu.make_async_copy(src, dst, sem)

    def issue(k, carry):
        t = i * tg + k
        tile_copy(t, p1_ref[t]).start()
        tile_copy(t, p2_ref[t]).start()
        return carry

    lax.fori_loop(0, tg, issue, 0)

    def drain(k, carry):
        tile_copy(0, 0).wait()
        tile_copy(0, 0).wait()
        return carry

    lax.fori_loop(0, tg, drain, 0)


def _dispatch(pos1, pos2, hn_t):
    rows8 = hn_t.shape[0]
    n = rows8 // SUBLANES
    grid_spec = pltpu.PrefetchScalarGridSpec(
        num_scalar_prefetch=2,
        grid=(n // ROW_TILE,),
        in_specs=[pl.BlockSpec(memory_space=pl.ANY)],
        out_specs=pl.BlockSpec(memory_space=pl.ANY),
        scratch_shapes=[pltpu.SemaphoreType.DMA(())],
    )
    return pl.pallas_call(
        _dispatch_kernel,
        grid_spec=grid_spec,
        out_shape=jax.ShapeDtypeStruct((2 * rows8, LANES), F32),
        compiler_params=_cparams(("arbitrary",)),
        name="moe_dispatch",
    )(pos1, pos2, hn_t)


def _combine_kernel(p1_ref, p2_ref, y_ref, h_ref, w1_ref, w2_ref, g_ref, o_ref, y1_ref, y2_ref, sem):
    i = pl.program_id(0)
    tc, d = h_ref.shape

    def tile_copy(pos, buf, k):
        src = y_ref.at[pl.ds(pl.multiple_of(pos * SUBLANES, SUBLANES), SUBLANES), :]
        dst = buf.at[pl.ds(pl.multiple_of(k * SUBLANES, SUBLANES), SUBLANES), :]
        return pltpu.make_async_copy(src, dst, sem)

    def issue(k, carry):
        t = i * tc + k
        tile_copy(p1_ref[t], y1_ref, k).start()
        tile_copy(p2_ref[t], y2_ref, k).start()
        return carry

    lax.fori_loop(0, tc, issue, 0)

    def drain(k, carry):
        tile_copy(0, y1_ref, 0).wait()
        tile_copy(0, y2_ref, 0).wait()
        return carry

    lax.fori_loop(0, tc, drain, 0)

    w1 = w1_ref[...]
    w2 = w2_ref[...]
    slabs = []
    ss = jnp.zeros((tc, 1), F32)
    for s in range(d // LANES):
        rows = pl.ds(s, tc, stride=SUBLANES)
        hs = h_ref[:, s * LANES:(s + 1) * LANES] + w1 * y1_ref[rows, :] + w2 * y2_ref[rows, :]
        ss = ss + jnp.sum(hs * hs, axis=-1, keepdims=True)
        slabs.append(hs)
    inv = lax.rsqrt(ss / d + EPS)
    for s, hs in enumerate(slabs):
        o_ref[:, s * LANES:(s + 1) * LANES] = hs * inv * g_ref[:, s * LANES:(s + 1) * LANES]


def _combine(pos1, pos2, y_t, h, w1, w2, g):
    n, d = h.shape
    tc = COMBINE_TILE
    row = lambda i, p1, p2: (i, 0)
    grid_spec = pltpu.PrefetchScalarGridSpec(
        num_scalar_prefetch=2,
        grid=(n // tc,),
        in_specs=[pl.BlockSpec(memory_space=pl.ANY),
                  pl.BlockSpec((tc, d), row),
                  pl.BlockSpec((tc, 1), row), pl.BlockSpec((tc, 1), row),
                  pl.BlockSpec((1, d), lambda i, p1, p2: (0, 0))],
        out_specs=pl.BlockSpec((tc, d), row),
        scratch_shapes=[pltpu.VMEM((tc * SUBLANES, LANES), F32), pltpu.VMEM((tc * SUBLANES, LANES), F32),
                        pltpu.SemaphoreType.DMA(())],
    )
    return pl.pallas_call(
        _combine_kernel,
        grid_spec=grid_spec,
        out_shape=jax.ShapeDtypeStruct((n, d), F32),
        compiler_params=_cparams(("arbitrary",)),
        name="moe_combine",
    )(pos1, pos2, y_t, h, w1, w2, g)


def _visit_plan(counts, n_slots):
    ne = counts.shape[0]
    tm = ROW_TILE
    n_tiles = n_slots // tm
    n_visits_max = n_tiles + ne - 1
    ends = jnp.cumsum(counts)
    starts = ends - counts
    t0 = jnp.arange(n_tiles, dtype=I32)[:, None] * tm
    lo = jnp.maximum(starts[None, :], t0)
    hi = jnp.minimum(ends[None, :], t0 + tm)
    hit = (hi > lo).reshape(-1)
    nv = jnp.sum(hit).astype(I32)
    idx = jnp.nonzero(hit, size=n_visits_max, fill_value=0)[0].astype(I32)
    last = idx[jnp.maximum(nv - 1, 0)]
    idx = jnp.where(jnp.arange(n_visits_max) < nv, idx, last)
    vt = idx // ne
    ve = idx % ne
    vlo = starts[ve]
    vhi = ends[ve]
    vfirst = jnp.concatenate([jnp.ones((1,), I32), (vt[1:] != vt[:-1]).astype(I32)])
    return (vt, ve, vlo.astype(I32), vhi.astype(I32), vfirst, nv.reshape(1)), n_visits_max


def _rope_tables(seq):
    half = A_HEAD_DIM // 2
    inv = 1.0 / (ROPE_THETA ** (jnp.arange(0, A_HEAD_DIM, 2, dtype=F32) / A_HEAD_DIM))
    ang = jnp.arange(seq, dtype=F32)[:, None] * inv[None, :]
    cos, sin = jnp.cos(ang), jnp.sin(ang)
    reps = LANES // A_HEAD_DIM
    cos_t = jnp.tile(jnp.concatenate([cos, cos], axis=-1), (1, reps))
    sin_t = jnp.tile(jnp.concatenate([-sin, sin], axis=-1), (1, reps))
    return cos_t, sin_t


def kernel(x, a_norm, a_w_qkv, a_b_qkv, a_sinks, a_w_o, f_norm, f_w_gate, f_w_up, f_w_down,
           b_norm, b_w_in, b_conv_w, b_a_log, b_dt_bias, b_o_norm, b_w_o,
           m_norm, m_w_router, m_w_gate, m_w_up, m_w_down, final_norm):
    batch, seq, d = x.shape
    n = batch * seq
    x2 = x.reshape(n, d)
    row = lambda v: v.reshape(1, -1).astype(F32)

    nq = A_HEADS * A_HEAD_DIM
    nkv = A_KV_HEADS * A_HEAD_DIM
    dup = jnp.repeat(jnp.arange(A_KV_HEADS), 2)[:, None] * A_HEAD_DIM + jnp.arange(A_HEAD_DIM)[None, :]
    cols = jnp.concatenate([jnp.arange(nq), nq + dup.reshape(-1), nq + nkv + dup.reshape(-1)])
    w_qkv = a_w_qkv[0][:, cols].astype(BF16)
    b_qkv = a_b_qkv[0][cols].reshape(1, -1)
    cos_t, sin_t = _rope_tables(seq)
    scale = A_HEAD_DIM ** -0.5
    q, k, v = _qkv_proj(x2, row(a_norm[0]), w_qkv, b_qkv, cos_t * scale, sin_t * scale, cos_t, sin_t, seq)
    att = _attention(q, k, v, a_sinks[0].astype(F32), batch, seq)
    h, hn = _proj_res_norm(att, a_w_o[0].astype(BF16), x2, row(f_norm[0]), tiled_rows=False)
    h, hn = _ffn_dense(hn, f_w_gate[0].astype(BF16), f_w_up[0].astype(BF16), f_w_down[0].astype(BF16),
                       h, row(b_norm[0]))

    nqkvz = 4 * B_HEADS * B_HEAD_DIM
    w_in = b_w_in[0]
    qkvz = _matmul(hn, w_in[:, :nqkvz].astype(BF16), 1024, F32)
    w_gates = jnp.pad(w_in[:, nqkvz:], ((0, 0), (0, LANES - 2 * B_HEADS))).astype(BF16)
    gates = _matmul(hn, w_gates, LANES, F32)
    gdn = _gated_deltanet(qkvz, gates, b_conv_w[0], b_a_log[0].astype(F32), b_dt_bias[0].astype(F32),
                          row(b_o_norm[0]), batch, seq)
    h, hn_t = _proj_res_norm(gdn, b_w_o[0].astype(BF16), h, row(m_norm[0]), tiled_rows=True)

    e1, e2, w1, w2, r1, r2, cnt = _route(hn_t, m_w_router[0].T.astype(F32))
    counts = cnt[:, 0].astype(I32)
    starts = jnp.cumsum(counts) - counts
    e1, e2, r1, r2 = (a.reshape(n) for a in (e1, e2, r1, r2))
    pos1 = starts[e1] + r1
    pos2 = starts[e2] + r2
    visits, n_visits_max = _visit_plan(counts, 2 * n)
    xs_t = _dispatch(pos1, pos2, hn_t)
    y_t = _ffn_moe(xs_t, m_w_gate[0].astype(BF16), m_w_up[0].astype(BF16), m_w_down[0].astype(BF16),
                   visits, n_visits_max)
    out = _combine(pos1, pos2, y_t, h, w1.reshape(n, 1), w2.reshape(n, 1), row(final_norm))
    return out.reshape(batch, seq, d)
```

```python
import functools
import math

import jax
import jax.numpy as jnp
from jax import lax
from jax.experimental import pallas as pl
from jax.experimental.pallas import tpu as pltpu

F32 = jnp.float32
BF16 = jnp.bfloat16
I32 = jnp.int32

EPS = 1e-6
LANES = 128
SUBLANES = 8
VMEM_LIMIT = 56 * 1024 * 1024

A_HEADS, A_KV_HEADS, A_HEAD_DIM = 16, 4, 64
WINDOW = 128
ROPE_THETA = 10000.0
B_HEADS, B_HEAD_DIM = 8, 128
CONV_WIDTH = 4
N_EXPERTS = 8
GDN_CHUNK = 128
GDN_GROUP = 4
NEG = -1e30

ROW_TILE = 512
FF_TILE = 512
COMBINE_TILE = 256


def _cparams(sem):
    return pltpu.CompilerParams(dimension_semantics=sem, vmem_limit_bytes=VMEM_LIMIT)


def _rms(x, g):
    var = jnp.mean(x * x, axis=-1, keepdims=True)
    return x * lax.rsqrt(var + EPS) * g


def _silu(x):
    return x * (1.0 / (1.0 + jnp.exp(-x)))


def _qkv_kernel(x_ref, g_ref, w_ref, b_ref, cq_ref, sq_ref, ck_ref, sk_ref, q_ref, k_ref, v_ref):
    tm = x_ref.shape[0]
    xn = _rms(x_ref[...], g_ref[...]).astype(BF16)
    lane = lax.broadcasted_iota(I32, (tm, LANES), 1)
    first_half = (lane % A_HEAD_DIM) < (A_HEAD_DIM // 2)

    def rope(xs, c, s):
        sw = jnp.where(first_half, pltpu.roll(xs, LANES - 32, 1), pltpu.roll(xs, 32, 1))
        return xs * c + sw * s

    nq = q_ref.shape[1] // LANES
    nk = k_ref.shape[1] // LANES
    nv = v_ref.shape[1] // LANES
    for s in range(0, nq + nk + nv, 2):
        acc = jnp.dot(xn, w_ref[:, s * LANES:(s + 2) * LANES], preferred_element_type=F32)
        acc = acc + b_ref[:, s * LANES:(s + 2) * LANES]
        for t in range(2):
            col = s + t
            part = acc[:, t * LANES:(t + 1) * LANES]
            if col < nq:
                q_ref[:, col * LANES:(col + 1) * LANES] = rope(part, cq_ref[...], sq_ref[...]).astype(BF16)
            elif col < nq + nk:
                c = col - nq
                k_ref[:, c * LANES:(c + 1) * LANES] = rope(part, ck_ref[...], sk_ref[...]).astype(BF16)
            else:
                c = col - nq - nk
                v_ref[:, c * LANES:(c + 1) * LANES] = part.astype(BF16)


def _qkv_proj(x2, g, w, b, cq, sq, ck, sk, seq):
    n, d = x2.shape
    tm = ROW_TILE
    nq = A_HEADS * A_HEAD_DIM
    nkv = 2 * A_KV_HEADS * A_HEAD_DIM
    tblk = seq // tm
    row = lambda i: (i, 0)
    const = lambda i: (0, 0)
    tab = lambda i: (i % tblk, 0)
    return pl.pallas_call(
        _qkv_kernel,
        grid=(n // tm,),
        in_specs=[
            pl.BlockSpec((tm, d), row),
            pl.BlockSpec((1, d), const),
            pl.BlockSpec((d, nq + 2 * nkv), const),
            pl.BlockSpec((1, nq + 2 * nkv), const),
            pl.BlockSpec((tm, LANES), tab), pl.BlockSpec((tm, LANES), tab),
            pl.BlockSpec((tm, LANES), tab), pl.BlockSpec((tm, LANES), tab),
        ],
        out_specs=[pl.BlockSpec((tm, nq), row), pl.BlockSpec((tm, nkv), row), pl.BlockSpec((tm, nkv), row)],
        out_shape=[jax.ShapeDtypeStruct((n, nq), BF16), jax.ShapeDtypeStruct((n, nkv), BF16),
                   jax.ShapeDtypeStruct((n, nkv), BF16)],
        compiler_params=_cparams(("parallel",)),
        name="qkv_rope",
    )(x2, g, w, b, cq, sq, ck, sk)


def _attn_kernel(sink_ref, q_ref, kp_ref, kc_ref, vp_ref, vc_ref, o_ref):
    n = pl.program_id(1)
    w = WINDOW
    group = A_HEADS // A_KV_HEADS
    rows = group * w
    lo = lax.broadcasted_iota(I32, (w, LANES), 1) < A_HEAD_DIM
    qi = lax.broadcasted_iota(I32, (rows, 2 * w), 0) % w
    kj = lax.broadcasted_iota(I32, (rows, 2 * w), 1)
    valid = (kj > qi) & (kj <= qi + w) & ((n > 0) | (kj >= w))
    hrow = lax.broadcasted_iota(I32, (rows, 1), 0) // w
    zero = jnp.zeros((w, LANES), BF16)
    for g in range(A_KV_HEADS):
        cs = slice(g * LANES, (g + 1) * LANES)
        kd = jnp.concatenate([kp_ref[:, cs], kc_ref[:, cs]], axis=0)
        vd = jnp.concatenate([vp_ref[:, cs], vc_ref[:, cs]], axis=0)
        parts = []
        for p in range(group // 2):
            j = g * (group // 2) + p
            q2 = q_ref[:, j * LANES:(j + 1) * LANES]
            parts.append(jnp.where(lo, q2, zero))
            parts.append(jnp.where(lo, zero, q2))
        qs = jnp.concatenate(parts, axis=0)
        s = lax.dot_general(qs, kd, (((1,), (1,)), ((), ())), preferred_element_type=F32)
        s = jnp.where(valid, s, NEG)
        sk = jnp.full((rows, 1), sink_ref[g * group], F32)
        for h in range(1, group):
            sk = jnp.where(hrow == h, sink_ref[g * group + h], sk)
        m = jnp.maximum(jnp.max(s, axis=-1, keepdims=True), sk)
        p_ = jnp.exp(s - m)
        den = jnp.sum(p_, axis=-1, keepdims=True) + jnp.exp(sk - m)
        o = jnp.dot(p_.astype(BF16), vd, preferred_element_type=F32) / den
        for p in range(group // 2):
            j = g * (group // 2) + p
            o2 = jnp.where(lo, o[2 * p * w:(2 * p + 1) * w], o[(2 * p + 1) * w:(2 * p + 2) * w])
            o_ref[:, j * LANES:(j + 1) * LANES] = o2.astype(BF16)


def _attention(q, k, v, sinks, batch, seq):
    n, dq = q.shape
    dkv = k.shape[1]
    nb = seq // WINDOW
    cur = lambda b, i: (b * nb + i, 0)
    prev = lambda b, i: (b * nb + jnp.maximum(i - 1, 0), 0)
    return pl.pallas_call(
        _attn_kernel,
        grid=(batch, nb),
        in_specs=[
            pl.BlockSpec(memory_space=pltpu.SMEM),
            pl.BlockSpec((WINDOW, dq), cur),
            pl.BlockSpec((WINDOW, dkv), prev), pl.BlockSpec((WINDOW, dkv), cur),
            pl.BlockSpec((WINDOW, dkv), prev), pl.BlockSpec((WINDOW, dkv), cur),
        ],
        out_specs=pl.BlockSpec((WINDOW, dq), cur),
        out_shape=jax.ShapeDtypeStruct((n, dq), BF16),
        compiler_params=_cparams(("parallel", "parallel")),
        name="swa_attention",
    )(sinks, q, k, k, v, v)


def _proj_kernel(a_ref, w_ref, res_ref, g_ref, h_ref, hn_ref, *, tiled_rows):
    h = res_ref[...] + jnp.dot(a_ref[...], w_ref[...], preferred_element_type=F32)
    h_ref[...] = h
    hn = _rms(h, g_ref[...])
    if tiled_rows:
        tm = h.shape[0]
        for s in range(h.shape[1] // LANES):
            hn_ref[pl.ds(s, tm, stride=SUBLANES), :] = hn[:, s * LANES:(s + 1) * LANES]
    else:
        hn_ref[...] = hn.astype(hn_ref.dtype)


def _proj_res_norm(a, w, res, g, *, tiled_rows):
    n, kdim = a.shape
    d = w.shape[1]
    tm = ROW_TILE
    row = lambda i: (i, 0)
    const = lambda i: (0, 0)
    if tiled_rows:
        assert d == SUBLANES * LANES
        hn_spec = pl.BlockSpec((tm * SUBLANES, LANES), row)
        hn_shape = jax.ShapeDtypeStruct((n * SUBLANES, LANES), F32)
    else:
        hn_spec = pl.BlockSpec((tm, d), row)
        hn_shape = jax.ShapeDtypeStruct((n, d), BF16)
    return pl.pallas_call(
        functools.partial(_proj_kernel, tiled_rows=tiled_rows),
        grid=(n // tm,),
        in_specs=[pl.BlockSpec((tm, kdim), row), pl.BlockSpec((kdim, d), const),
                  pl.BlockSpec((tm, d), row), pl.BlockSpec((1, d), const)],
        out_specs=[pl.BlockSpec((tm, d), row), hn_spec],
        out_shape=[jax.ShapeDtypeStruct((n, d), F32), hn_shape],
        compiler_params=_cparams(("parallel",)),
        name="proj_res_norm",
    )(a, w, res, g)


def _swiglu_partial(xb, wg, wu, wd):
    hg = jnp.dot(xb, wg, preferred_element_type=F32)
    hu = jnp.dot(xb, wu, preferred_element_type=F32)
    act = (_silu(hg) * hu).astype(BF16)
    return jnp.dot(act, wd, preferred_element_type=F32)


def _ffn_dense_kernel(x_ref, wg_ref, wu_ref, wd_ref, res_ref, g_ref, h_ref, hn_ref, acc_ref):
    j = pl.program_id(1)

    @pl.when(j == 0)
    def _():
        acc_ref[...] = res_ref[...]

    acc_ref[...] += _swiglu_partial(x_ref[...], wg_ref[...], wu_ref[...], wd_ref[...])

    @pl.when(j == pl.num_programs(1) - 1)
    def _():
        h = acc_ref[...]
        h_ref[...] = h
        hn_ref[...] = _rms(h, g_ref[...]).astype(hn_ref.dtype)


def _ffn_dense(xn, wg, wu, wd, res, g):
    n, d = xn.shape
    ff = wg.shape[1]
    tm, tf = ROW_TILE, FF_TILE
    row = lambda i, j: (i, 0)
    return pl.pallas_call(
        _ffn_dense_kernel,
        grid=(n // tm, ff // tf),
        in_specs=[
            pl.BlockSpec((tm, d), row),
            pl.BlockSpec((d, tf), lambda i, j: (0, j)),
            pl.BlockSpec((d, tf), lambda i, j: (0, j)),
            pl.BlockSpec((tf, d), lambda i, j: (j, 0)),
            pl.BlockSpec((tm, d), row),
            pl.BlockSpec((1, d), lambda i, j: (0, 0)),
        ],
        out_specs=[pl.BlockSpec((tm, d), row), pl.BlockSpec((tm, d), row)],
        out_shape=[jax.ShapeDtypeStruct((n, d), F32), jax.ShapeDtypeStruct((n, d), BF16)],
        scratch_shapes=[pltpu.VMEM((tm, d), F32)],
        compiler_params=_cparams(("parallel", "arbitrary")),
        name="swiglu_dense",
    )(xn, wg, wu, wd, res, g)


def _ffn_moe_kernel(vt_ref, ve_ref, vlo_ref, vhi_ref, vfirst_ref, nv_ref,
                    x_ref, wg_ref, wu_ref, wd_ref, y_ref, acc_ref, xb_ref):
    v = pl.program_id(0)
    j = pl.program_id(1)
    tm, d = acc_ref.shape
    nslab = d // LANES

    @pl.when(v < nv_ref[0])
    def _():
        @pl.when(j == 0)
        def _():
            slabs = [x_ref[pl.ds(s, tm, stride=SUBLANES), :] for s in range(nslab)]
            xb_ref[...] = jnp.concatenate(slabs, axis=-1).astype(BF16)
            acc_ref[...] = jnp.zeros_like(acc_ref)

        acc_ref[...] += _swiglu_partial(xb_ref[...], wg_ref[...], wu_ref[...], wd_ref[...])

        @pl.when(j == pl.num_programs(1) - 1)
        def _():
            r = vt_ref[v] * tm + lax.broadcasted_iota(I32, (tm, 1), 0)
            mine = (r >= vlo_ref[v]) & (r < vhi_ref[v])

            @pl.when(vfirst_ref[v] == 1)
            def _():
                for s in range(nslab):
                    new = acc_ref[:, s * LANES:(s + 1) * LANES]
                    y_ref[pl.ds(s, tm, stride=SUBLANES), :] = jnp.where(mine, new, 0.0)

            @pl.when(vfirst_ref[v] != 1)
            def _():
                for s in range(nslab):
                    rows = pl.ds(s, tm, stride=SUBLANES)
                    new = acc_ref[:, s * LANES:(s + 1) * LANES]
                    y_ref[rows, :] = jnp.where(mine, new, y_ref[rows, :])


def _ffn_moe(xs_t, wg, wu, wd, visits, n_visits_max):
    vt, ve, vlo, vhi, vfirst, nv = visits
    rows8, _ = xs_t.shape
    p = rows8 // SUBLANES
    d = wg.shape[1]
    ff = wg.shape[2]
    tm, tf = ROW_TILE, FF_TILE
    nf = ff // tf

    def jj(v, j, nv_ref):
        return jnp.where(v < nv_ref[0], j, nf - 1)

    xmap = lambda v, j, vt, ve, vlo, vhi, vf, nv: (vt[v], 0)
    gmap = lambda v, j, vt, ve, vlo, vhi, vf, nv: (ve[v], 0, jj(v, j, nv))
    dmap = lambda v, j, vt, ve, vlo, vhi, vf, nv: (ve[v], jj(v, j, nv), 0)
    grid_spec = pltpu.PrefetchScalarGridSpec(
        num_scalar_prefetch=6,
        grid=(n_visits_max, nf),
        in_specs=[
            pl.BlockSpec((tm * SUBLANES, LANES), xmap),
            pl.BlockSpec((None, d, tf), gmap),
            pl.BlockSpec((None, d, tf), gmap),
            pl.BlockSpec((None, tf, d), dmap),
        ],
        out_specs=pl.BlockSpec((tm * SUBLANES, LANES), xmap),
        scratch_shapes=[pltpu.VMEM((tm, d), F32), pltpu.VMEM((tm, d), BF16)],
    )
    return pl.pallas_call(
        _ffn_moe_kernel,
        grid_spec=grid_spec,
        out_shape=jax.ShapeDtypeStruct((p * SUBLANES, LANES), F32),
        compiler_params=_cparams(("arbitrary", "arbitrary")),
        name="swiglu_moe",
    )(vt, ve, vlo, vhi, vfirst, nv, xs_t, wg, wu, wd)


def _mm_kernel(x_ref, w_ref, o_ref):
    o_ref[...] = jnp.dot(x_ref[...], w_ref[...], preferred_element_type=F32).astype(o_ref.dtype)


def _matmul(x, w, tn, out_dtype):
    n, kdim = x.shape
    m = w.shape[1]
    tm = ROW_TILE
    return pl.pallas_call(
        _mm_kernel,
        grid=(m // tn, n // tm),
        in_specs=[pl.BlockSpec((tm, kdim), lambda c, r: (r, 0)), pl.BlockSpec((kdim, tn), lambda c, r: (0, c))],
        out_specs=pl.BlockSpec((tm, tn), lambda c, r: (r, c)),
        out_shape=jax.ShapeDtypeStruct((n, m), out_dtype),
        compiler_params=_cparams(("parallel", "parallel")),
        name="matmul",
    )(x, w)


def _gdn_kernel(alog_ref, dtb_ref, q_ref, k_ref, v_ref, z_ref, gt_ref, cwq_ref, cwk_ref, cwv_ref, on_ref,
                o_ref, u_ref, wq_ref, in_ref, kd_ref, gl_ref):
    h = pl.program_id(1)
    t, dh = q_ref.shape
    c = GDN_CHUNK
    grp = GDN_GROUP
    ngroups = t // (c * grp)
    halo = SUBLANES
    ii = lax.broadcasted_iota(I32, (c, c), 0)
    jx = lax.broadcasted_iota(I32, (c, c), 1)
    causal = ii >= jx
    strict = ii > jx
    bits = ii ^ jx
    eye = jnp.where(ii == jx, 1.0, 0.0)
    pos = lax.broadcasted_iota(I32, (c, dh), 0)
    nt = (((1,), (1,)), ((), ()))
    tn = (((0,), (0,)), ((), ()))
    neg_a = -jnp.exp(alog_ref[h])
    dtb = dtb_ref[h]

    def conv_silu(x_ref, w_ref, ci, rs, rp):
        x = x_ref[rs, :]
        prev = jnp.where(ci > 0, x_ref[rp, :], 0.0)
        xx = jnp.concatenate([prev, x], axis=0)
        w = w_ref[...]
        y = x * w[CONV_WIDTH - 1:CONV_WIDTH, :]
        for s in range(1, CONV_WIDTH):
            y = y + pltpu.roll(xx, s, 0)[halo:] * w[CONV_WIDTH - 1 - s:CONV_WIDTH - s, :]
        return _silu(y)

    def l2n(x):
        return x * lax.rsqrt(jnp.sum(x * x, axis=-1, keepdims=True) + EPS)

    def local_terms(ci):
        r0 = pl.multiple_of(ci * c, c)
        rs = pl.ds(r0, c)
        rp = pl.ds(pl.multiple_of(jnp.maximum(r0 - halo, 0), halo), halo)
        qc = l2n(conv_silu(q_ref, cwq_ref, ci, rs, rp)) * (dh ** -0.5)
        kc = l2n(conv_silu(k_ref, cwk_ref, ci, rs, rp))
        vc = conv_silu(v_ref, cwv_ref, ci, rs, rp)
        gt = gt_ref[rs, :]
        lane = lax.broadcasted_iota(I32, gt.shape, 1)
        b_log = jnp.sum(jnp.where(lane == h, gt, 0.0), axis=-1, keepdims=True)
        a_log = jnp.sum(jnp.where(lane == h + B_HEADS, gt, 0.0), axis=-1, keepdims=True)
        beta = jnp.broadcast_to(1.0 / (1.0 + jnp.exp(-b_log)), (c, dh))
        xa = a_log + dtb
        softplus = jnp.maximum(xa, 0.0) + jnp.log(1.0 + jnp.exp(-jnp.abs(xa)))
        gcb = jnp.broadcast_to(neg_a * softplus, (c, dh))
        sh = 1
        while sh < c:
            gcb = gcb + jnp.where(pos >= sh, pltpu.roll(gcb, sh, 0), 0.0)
            sh *= 2
        return qc, kc, vc, beta, gcb

    def prepare(cis):
        terms = [local_terms(ci) for ci in cis]
        yield
        lowers, intras, kbs, egs = [], [], [], []
        for qc, kc, vc, beta, gcb in terms:
            diff = gcb - gcb.T
            decay = jnp.exp(jnp.where(causal, diff, NEG))
            kb = kc * beta
            both = lax.dot_general(jnp.concatenate([kb.astype(BF16), qc.astype(BF16)], axis=0),
                                   kc.astype(BF16), nt, preferred_element_type=F32)
            lowers.append(jnp.where(strict, both[:c] * decay, 0.0))
            intras.append(jnp.where(causal, both[c:] * decay, 0.0))
            kbs.append(kb)
            egs.append(jnp.exp(gcb))
        yield
        tinvs = [eye - jnp.where(bits < 2, lower, 0.0) for lower in lowers]
        m = 2
        while m < c:
            level = (bits >= m) & (bits < 2 * m)
            tbs = [tinv.astype(BF16) for tinv in tinvs]
            ys = [jnp.dot(jnp.where(level, lower, 0.0).astype(BF16), tb, preferred_element_type=F32).astype(BF16)
                  for lower, tb in zip(lowers, tbs)]
            yield
            tinvs = [tinv - jnp.dot(tb, y, preferred_element_type=F32) for tinv, tb, y in zip(tinvs, tbs, ys)]
            yield
            m *= 2
        for ci, (qc, kc, vc, beta, gcb), tinv, intra, kb, eg in zip(cis, terms, tinvs, intras, kbs, egs):
            r0 = pl.multiple_of(ci * c, c)
            rs = pl.ds(r0, c)
            rhs = jnp.concatenate([vc * beta, kb * eg], axis=-1)
            x = rhs + jnp.dot((tinv - eye).astype(BF16), rhs.astype(BF16), preferred_element_type=F32)
            glast = gcb[c - 1:c, :]
            u_ref[rs, :] = x[:, :dh]
            wq_ref[pl.ds(pl.multiple_of(2 * r0, c), c), :] = x[:, dh:].astype(BF16)
            wq_ref[pl.ds(pl.multiple_of(2 * r0 + c, c), c), :] = (qc * eg).astype(BF16)
            in_ref[rs, :] = intra.astype(BF16)
            kd_ref[rs, :] = (kc * jnp.exp(glast - gcb)).astype(BF16)
            gl_ref[pl.ds(pl.multiple_of(ci * halo, halo), halo), :] = jnp.broadcast_to(jnp.exp(glast), (halo, dh))
        yield

    def scan(ci, state):
        r0 = pl.multiple_of(ci * c, c)
        rs = pl.ds(r0, c)
        xs = jnp.dot(wq_ref[pl.ds(pl.multiple_of(2 * r0, c), 2 * c), :], state.astype(BF16),
                     preferred_element_type=F32)
        vnew = (u_ref[rs, :] - xs[:c]).astype(BF16)
        o = xs[c:] + jnp.dot(in_ref[rs, :], vnew, preferred_element_type=F32)
        dec = gl_ref[pl.ds(pl.multiple_of(ci * halo, halo), 1), :]
        state = state * dec + lax.dot_general(kd_ref[rs, :], vnew, tn, preferred_element_type=F32)
        on = _rms(o, on_ref[...]) * _silu(z_ref[rs, :])
        o_ref[rs, :] = on.astype(o_ref.dtype)
        return state

    for _ in prepare([jnp.int32(k) for k in range(grp)]):
        pass

    def body(g, state):
        stages = prepare([g * grp + k for k in range(grp)])
        n_stages = 2 * (c.bit_length() - 2) + 3
        done = 0
        for k in range(grp):
            while done < (k * n_stages) // grp:
                next(stages)
                done += 1
            state = scan((g - 1) * grp + k, state)
        for _ in stages:
            pass
        return state

    state = lax.fori_loop(1, ngroups, body, jnp.zeros((dh, dh), F32))
    for k in range(grp):
        state = scan(jnp.int32((ngroups - 1) * grp + k), state)


def _gated_deltanet(qkvz, gates, conv_w, a_log, dt_bias, o_norm, batch, seq):
    n = qkvz.shape[0]
    dh = B_HEAD_DIM
    nh = B_HEADS
    c = GDN_CHUNK
    assert seq % (c * GDN_GROUP) == 0
    blk = lambda off: pl.BlockSpec((seq, dh), lambda b, h, off=off: (b, off + h))
    cw = lambda off: pl.BlockSpec((CONV_WIDTH, dh), lambda b, h, off=off: (0, off + h))
    smem = pl.BlockSpec(memory_space=pltpu.SMEM)
    scratch = [pltpu.VMEM((seq, dh), F32), pltpu.VMEM((2 * seq, dh), BF16), pltpu.VMEM((seq, c), BF16),
               pltpu.VMEM((seq, dh), BF16), pltpu.VMEM((seq // c * SUBLANES, dh), F32)]
    return pl.pallas_call(
        _gdn_kernel,
        grid=(batch, nh),
        in_specs=[smem, smem, blk(0), blk(nh), blk(2 * nh), blk(3 * nh),
                  pl.BlockSpec((seq, LANES), lambda b, h: (b, 0)),
                  cw(0), cw(nh), cw(2 * nh),
                  pl.BlockSpec((1, dh), lambda b, h: (0, 0))],
        out_specs=pl.BlockSpec((seq, dh), lambda b, h: (b, h)),
        out_shape=jax.ShapeDtypeStruct((n, nh * dh), BF16),
        scratch_shapes=scratch,
        compiler_params=_cparams(("parallel", "parallel")),
        name="gated_deltanet",
    )(a_log, dt_bias, qkvz, qkvz, qkvz, qkvz, gates, conv_w, conv_w, conv_w, o_norm)


def _route_kernel(hn_ref, wr_ref, e1_ref, e2_ref, w1_ref, w2_ref, r1_ref, r2_ref, cnt_ref, base_ref):
    i = pl.program_id(0)
    tm, d = hn_ref.shape[0] // SUBLANES, SUBLANES * LANES
    ne = wr_ref.shape[0]

    @pl.when(i == 0)
    def _():
        base_ref[...] = jnp.zeros_like(base_ref)

    lg = jnp.zeros((ne, tm), F32)
    for s in range(SUBLANES):
        lg = lg + lax.dot_general(wr_ref[:, s * LANES:(s + 1) * LANES], hn_ref[pl.ds(s, tm, stride=SUBLANES), :],
                                  (((1,), (1,)), ((), ())), precision=lax.Precision.HIGHEST,
                                  preferred_element_type=F32)
    eio = lax.broadcasted_iota(I32, (ne, tm), 0)
    m1 = jnp.max(lg, axis=0, keepdims=True)
    i1 = jnp.min(jnp.where(lg == m1, eio, ne), axis=0, keepdims=True)
    lg2 = jnp.where(eio == i1, -jnp.inf, lg)
    m2 = jnp.max(lg2, axis=0, keepdims=True)
    i2 = jnp.min(jnp.where(lg2 == m2, eio, ne), axis=0, keepdims=True)
    ex = jnp.exp(m2 - m1)
    w1 = 1.0 / (1.0 + ex)
    sel = jnp.where((eio == i1) | (eio == i2), 1.0, 0.0)
    tri = jnp.where(lax.broadcasted_iota(I32, (tm, tm), 0) < lax.broadcasted_iota(I32, (tm, tm), 1), 1.0, 0.0)
    pre = jnp.dot(sel.astype(BF16), tri.astype(BF16), preferred_element_type=F32) + base_ref[:, 0:1]
    r1 = jnp.sum(jnp.where(eio == i1, pre, 0.0), axis=0, keepdims=True)
    r2 = jnp.sum(jnp.where(eio == i2, pre, 0.0), axis=0, keepdims=True)
    base_ref[...] += jnp.sum(sel, axis=1, keepdims=True)
    e1_ref[0] = i1
    e2_ref[0] = i2
    w1_ref[0] = w1
    w2_ref[0] = ex * w1
    r1_ref[0] = r1.astype(I32)
    r2_ref[0] = r2.astype(I32)
    cnt_ref[...] = base_ref[...]


def _route(hn_t, wr_t):
    n = hn_t.shape[0] // SUBLANES
    ne = wr_t.shape[0]
    tm = ROW_TILE
    nt = n // tm
    vec = pl.BlockSpec((1, 1, tm), lambda i: (i, 0, 0))
    ish = jax.ShapeDtypeStruct((nt, 1, tm), I32)
    fsh = jax.ShapeDtypeStruct((nt, 1, tm), F32)
    return pl.pallas_call(
        _route_kernel,
        grid=(nt,),
        in_specs=[pl.BlockSpec((tm * SUBLANES, LANES), lambda i: (i, 0)),
                  pl.BlockSpec((ne, SUBLANES * LANES), lambda i: (0, 0))],
        out_specs=[vec, vec, vec, vec, vec, vec, pl.BlockSpec((ne, LANES), lambda i: (0, 0))],
        out_shape=[ish, ish, fsh, fsh, ish, ish, jax.ShapeDtypeStruct((ne, LANES), F32)],
        scratch_shapes=[pltpu.VMEM((ne, LANES), F32)],
        compiler_params=_cparams(("arbitrary",)),
        name="moe_route",
    )(hn_t, wr_t)


def _dispatch_kernel(p1_ref, p2_ref, hn_ref, xs_ref, sem):
    i = pl.program_id(0)
    tg = hn_ref.shape[0] // SUBLANES

    def tile_copy(k, pos):
        src = hn_ref.at[pl.ds(pl.multiple_of(k * SUBLANES, SUBLANES), SUBLANES), :]
        dst = xs_ref.at[pl.ds(pl.multiple_of(pos * SUBLANES, SUBLANES), SUBLANES), :]
        return pltpu.make_async_copy(src, dst, sem)

    def issue(k, carry):
        t = i * tg + k
        tile_copy(k, p1_ref[t]).start()
        tile_copy(k, p2_ref[t]).start()
        return carry

    lax.fori_loop(0, tg, issue, 0, unroll=8)

    def drain(k, carry):
        tile_copy(0, 0).wait()
        tile_copy(0, 0).wait()
        return carry

    lax.fori_loop(0, tg, drain, 0, unroll=8)


def _dispatch(pos1, pos2, hn_t):
    rows8 = hn_t.shape[0]
    n = rows8 // SUBLANES
    tg = ROW_TILE
    grid_spec = pltpu.PrefetchScalarGridSpec(
        num_scalar_prefetch=2,
        grid=(n // tg,),
        in_specs=[pl.BlockSpec((tg * SUBLANES, LANES), lambda i, p1, p2: (i, 0))],
        out_specs=pl.BlockSpec(memory_space=pl.ANY),
        scratch_shapes=[pltpu.SemaphoreType.DMA(())],
    )
    return pl.pallas_call(
        _dispatch_kernel,
        grid_spec=grid_spec,
        out_shape=jax.ShapeDtypeStruct((2 * rows8, LANES), F32),
        compiler_params=_cparams(("arbitrary",)),
        name="moe_dispatch",
    )(pos1, pos2, hn_t)


def _combine_kernel(p1_ref, p2_ref, y_ref, h_ref, w1_ref, w2_ref, g_ref, o_ref, y1_ref, y2_ref, sem):
    i = pl.program_id(0)
    tc, d = h_ref.shape

    def tile_copy(pos, buf, k):
        src = y_ref.at[pl.ds(pl.multiple_of(pos * SUBLANES, SUBLANES), SUBLANES), :]
        dst = buf.at[pl.ds(pl.multiple_of(k * SUBLANES, SUBLANES), SUBLANES), :]
        return pltpu.make_async_copy(src, dst, sem)

    def issue(k, carry):
        t = i * tc + k
        tile_copy(p1_ref[t], y1_ref, k).start()
        tile_copy(p2_ref[t], y2_ref, k).start()
        return carry

    lax.fori_loop(0, tc, issue, 0, unroll=8)

    def drain(k, carry):
        tile_copy(0, y1_ref, 0).wait()
        tile_copy(0, y2_ref, 0).wait()
        return carry

    lax.fori_loop(0, tc, drain, 0, unroll=8)

    w1 = w1_ref[...]
    w2 = w2_ref[...]
    slabs = []
    ss = jnp.zeros((tc, 1), F32)
    for s in range(d // LANES):
        rows = pl.ds(s, tc, stride=SUBLANES)
        hs = h_ref[:, s * LANES:(s + 1) * LANES] + w1 * y1_ref[rows, :] + w2 * y2_ref[rows, :]
        ss = ss + jnp.sum(hs * hs, axis=-1, keepdims=True)
        slabs.append(hs)
    inv = lax.rsqrt(ss / d + EPS)
    for s, hs in enumerate(slabs):
        o_ref[:, s * LANES:(s + 1) * LANES] = hs * inv * g_ref[:, s * LANES:(s + 1) * LANES]


def _combine(pos1, pos2, y_t, h, w1, w2, g):
    n, d = h.shape
    tc = COMBINE_TILE
    row = lambda i, p1, p2: (i, 0)
    grid_spec = pltpu.PrefetchScalarGridSpec(
        num_scalar_prefetch=2,
        grid=(n // tc,),
        in_specs=[pl.BlockSpec(memory_space=pl.ANY),
                  pl.BlockSpec((tc, d), row),
                  pl.BlockSpec((tc, 1), row), pl.BlockSpec((tc, 1), row),
                  pl.BlockSpec((1, d), lambda i, p1, p2: (0, 0))],
        out_specs=pl.BlockSpec((tc, d), row),
        scratch_shapes=[pltpu.VMEM((tc * SUBLANES, LANES), F32), pltpu.VMEM((tc * SUBLANES, LANES), F32),
                        pltpu.SemaphoreType.DMA(())],
    )
    return pl.pallas_call(
        _combine_kernel,
        grid_spec=grid_spec,
        out_shape=jax.ShapeDtypeStruct((n, d), F32),
        compiler_params=_cparams(("arbitrary",)),
        name="moe_combine",
    )(pos1, pos2, y_t, h, w1, w2, g)


def _visit_plan(counts, n_slots):
    ne = counts.shape[0]
    tm = ROW_TILE
    n_tiles = n_slots // tm
    n_visits_max = n_tiles + ne - 1
    ends = jnp.cumsum(counts)
    starts = ends - counts
    t0 = jnp.arange(n_tiles, dtype=I32)[:, None] * tm
    lo = jnp.maximum(starts[None, :], t0)
    hi = jnp.minimum(ends[None, :], t0 + tm)
    hit = (hi > lo).reshape(-1)
    nv = jnp.sum(hit).astype(I32)
    idx = jnp.nonzero(hit, size=n_visits_max, fill_value=0)[0].astype(I32)
    last = idx[jnp.maximum(nv - 1, 0)]
    idx = jnp.where(jnp.arange(n_visits_max) < nv, idx, last)
    vt = idx // ne
    ve = idx % ne
    vlo = starts[ve]
    vhi = ends[ve]
    vfirst = jnp.concatenate([jnp.ones((1,), I32), (vt[1:] != vt[:-1]).astype(I32)])
    return (vt, ve, vlo.astype(I32), vhi.astype(I32), vfirst, nv.reshape(1)), n_visits_max


def _rope_tables(seq):
    half = A_HEAD_DIM // 2
    inv = 1.0 / (ROPE_THETA ** (jnp.arange(0, A_HEAD_DIM, 2, dtype=F32) / A_HEAD_DIM))
    ang = jnp.arange(seq, dtype=F32)[:, None] * inv[None, :]
    cos, sin = jnp.cos(ang), jnp.sin(ang)
    reps = LANES // A_HEAD_DIM
    cos_t = jnp.tile(jnp.concatenate([cos, cos], axis=-1), (1, reps))
    sin_t = jnp.tile(jnp.concatenate([-sin, sin], axis=-1), (1, reps))
    return cos_t, sin_t


def kernel(x, a_norm, a_w_qkv, a_b_qkv, a_sinks, a_w_o, f_norm, f_w_gate, f_w_up, f_w_down,
           b_norm, b_w_in, b_conv_w, b_a_log, b_dt_bias, b_o_norm, b_w_o,
           m_norm, m_w_router, m_w_gate, m_w_up, m_w_down, final_norm):
    batch, seq, d = x.shape
    n = batch * seq
    x2 = x.reshape(n, d)
    row = lambda v: v.reshape(1, -1).astype(F32)

    nq = A_HEADS * A_HEAD_DIM
    nkv = A_KV_HEADS * A_HEAD_DIM
    dup = jnp.repeat(jnp.arange(A_KV_HEADS), 2)[:, None] * A_HEAD_DIM + jnp.arange(A_HEAD_DIM)[None, :]
    cols = jnp.concatenate([jnp.arange(nq), nq + dup.reshape(-1), nq + nkv + dup.reshape(-1)])
    w_qkv = a_w_qkv[0][:, cols].astype(BF16)
    b_qkv = a_b_qkv[0][cols].reshape(1, -1)
    cos_t, sin_t = _rope_tables(seq)
    scale = A_HEAD_DIM ** -0.5
    q, k, v = _qkv_proj(x2, row(a_norm[0]), w_qkv, b_qkv, cos_t * scale, sin_t * scale, cos_t, sin_t, seq)
    att = _attention(q, k, v, a_sinks[0].astype(F32), batch, seq)
    h, hn = _proj_res_norm(att, a_w_o[0].astype(BF16), x2, row(f_norm[0]), tiled_rows=False)
    h, hn = _ffn_dense(hn, f_w_gate[0].astype(BF16), f_w_up[0].astype(BF16), f_w_down[0].astype(BF16),
                       h, row(b_norm[0]))

    nqkvz = 4 * B_HEADS * B_HEAD_DIM
    w_in = b_w_in[0]
    qkvz = _matmul(hn, w_in[:, :nqkvz].astype(BF16), 1024, F32)
    w_gates = jnp.pad(w_in[:, nqkvz:], ((0, 0), (0, LANES - 2 * B_HEADS))).astype(BF16)
    gates = _matmul(hn, w_gates, LANES, F32)
    gdn = _gated_deltanet(qkvz, gates, b_conv_w[0], b_a_log[0].astype(F32), b_dt_bias[0].astype(F32),
                          row(b_o_norm[0]), batch, seq)
    h, hn_t = _proj_res_norm(gdn, b_w_o[0].astype(BF16), h, row(m_norm[0]), tiled_rows=True)

    e1, e2, w1, w2, r1, r2, cnt = _route(hn_t, m_w_router[0].T.astype(F32))
    counts = cnt[:, 0].astype(I32)
    starts = jnp.cumsum(counts) - counts
    e1, e2, r1, r2 = (a.reshape(n) for a in (e1, e2, r1, r2))
    pos1 = starts[e1] + r1
    pos2 = starts[e2] + r2
    visits, n_visits_max = _visit_plan(counts, 2 * n)
    xs_t = _dispatch(pos1, pos2, hn_t)
    y_t = _ffn_moe(xs_t, m_w_gate[0].astype(BF16), m_w_up[0].astype(BF16), m_w_down[0].astype(BF16),
                   visits, n_visits_max)
    out = _combine(pos1, pos2, y_t, h, w1.reshape(n, 1), w2.reshape(n, 1), row(final_norm))
    return out.reshape(batch, seq, d)
```

```python
import functools
import math

import jax
import jax.numpy as jnp
from jax import lax
from jax.experimental import pallas as pl
from jax.experimental.pallas import tpu as pltpu

F32 = jnp.float32
BF16 = jnp.bfloat16
I32 = jnp.int32

EPS = 1e-6
LANES = 128
SUBLANES = 8
VMEM_LIMIT = 56 * 1024 * 1024

A_HEADS, A_KV_HEADS, A_HEAD_DIM = 16, 4, 64
WINDOW = 128
ROPE_THETA = 10000.0
B_HEADS, B_HEAD_DIM = 8, 128
CONV_WIDTH = 4
N_EXPERTS = 8
GDN_CHUNK = 128
GDN_GROUP = 4
NEG = -1e30

ROW_TILE = 512
FF_TILE = 1792
COMBINE_TILE = 256


def _cparams(sem):
    return pltpu.CompilerParams(dimension_semantics=sem, vmem_limit_bytes=VMEM_LIMIT)


def _rms(x, g):
    var = jnp.mean(x * x, axis=-1, keepdims=True)
    return x * lax.rsqrt(var + EPS) * g


def _silu(x):
    return x * (1.0 / (1.0 + jnp.exp(-x)))


def _qkv_kernel(x_ref, g_ref, w_ref, b_ref, cq_ref, sq_ref, ck_ref, sk_ref, q_ref, k_ref, v_ref):
    tm = x_ref.shape[0]
    xn = _rms(x_ref[...], g_ref[...]).astype(BF16)
    lane = lax.broadcasted_iota(I32, (tm, LANES), 1)
    first_half = (lane % A_HEAD_DIM) < (A_HEAD_DIM // 2)

    def rope(xs, c, s):
        sw = jnp.where(first_half, pltpu.roll(xs, LANES - 32, 1), pltpu.roll(xs, 32, 1))
        return xs * c + sw * s

    nq = q_ref.shape[1] // LANES
    nk = k_ref.shape[1] // LANES
    nv = v_ref.shape[1] // LANES
    for s in range(0, nq + nk + nv, 2):
        acc = jnp.dot(xn, w_ref[:, s * LANES:(s + 2) * LANES], preferred_element_type=F32)
        acc = acc + b_ref[:, s * LANES:(s + 2) * LANES]
        for t in range(2):
            col = s + t
            part = acc[:, t * LANES:(t + 1) * LANES]
            if col < nq:
                q_ref[:, col * LANES:(col + 1) * LANES] = rope(part, cq_ref[...], sq_ref[...]).astype(BF16)
            elif col < nq + nk:
                c = col - nq
                k_ref[:, c * LANES:(c + 1) * LANES] = rope(part, ck_ref[...], sk_ref[...]).astype(BF16)
            else:
                c = col - nq - nk
                v_ref[:, c * LANES:(c + 1) * LANES] = part.astype(BF16)


def _qkv_proj(x2, g, w, b, cq, sq, ck, sk, seq):
    n, d = x2.shape
    tm = ROW_TILE
    nq = A_HEADS * A_HEAD_DIM
    nkv = 2 * A_KV_HEADS * A_HEAD_DIM
    tblk = seq // tm
    row = lambda i: (i, 0)
    const = lambda i: (0, 0)
    tab = lambda i: (i % tblk, 0)
    return pl.pallas_call(
        _qkv_kernel,
        grid=(n // tm,),
        in_specs=[
            pl.BlockSpec((tm, d), row),
            pl.BlockSpec((1, d), const),
            pl.BlockSpec((d, nq + 2 * nkv), const),
            pl.BlockSpec((1, nq + 2 * nkv), const),
            pl.BlockSpec((tm, LANES), tab), pl.BlockSpec((tm, LANES), tab),
            pl.BlockSpec((tm, LANES), tab), pl.BlockSpec((tm, LANES), tab),
        ],
        out_specs=[pl.BlockSpec((tm, nq), row), pl.BlockSpec((tm, nkv), row), pl.BlockSpec((tm, nkv), row)],
        out_shape=[jax.ShapeDtypeStruct((n, nq), BF16), jax.ShapeDtypeStruct((n, nkv), BF16),
                   jax.ShapeDtypeStruct((n, nkv), BF16)],
        compiler_params=_cparams(("parallel",)),
        name="qkv_rope",
    )(x2, g, w, b, cq, sq, ck, sk)


def _attn_kernel(sink_ref, q_ref, kp_ref, kc_ref, vp_ref, vc_ref, o_ref):
    n = pl.program_id(1)
    w = WINDOW
    group = A_HEADS // A_KV_HEADS
    rows = group * w
    lo = lax.broadcasted_iota(I32, (w, LANES), 1) < A_HEAD_DIM
    qi = lax.broadcasted_iota(I32, (rows, 2 * w), 0) % w
    kj = lax.broadcasted_iota(I32, (rows, 2 * w), 1)
    valid = (kj > qi) & (kj <= qi + w) & ((n > 0) | (kj >= w))
    hrow = lax.broadcasted_iota(I32, (rows, 1), 0) // w
    zero = jnp.zeros((w, LANES), BF16)
    for g in range(A_KV_HEADS):
        cs = slice(g * LANES, (g + 1) * LANES)
        kd = jnp.concatenate([kp_ref[:, cs], kc_ref[:, cs]], axis=0)
        vd = jnp.concatenate([vp_ref[:, cs], vc_ref[:, cs]], axis=0)
        parts = []
        for p in range(group // 2):
            j = g * (group // 2) + p
            q2 = q_ref[:, j * LANES:(j + 1) * LANES]
            parts.append(jnp.where(lo, q2, zero))
            parts.append(jnp.where(lo, zero, q2))
        qs = jnp.concatenate(parts, axis=0)
        s = lax.dot_general(qs, kd, (((1,), (1,)), ((), ())), preferred_element_type=F32)
        s = jnp.where(valid, s, NEG)
        sk = jnp.full((rows, 1), sink_ref[g * group], F32)
        for h in range(1, group):
            sk = jnp.where(hrow == h, sink_ref[g * group + h], sk)
        m = jnp.maximum(jnp.max(s, axis=-1, keepdims=True), sk)
        p_ = jnp.exp(s - m)
        den = jnp.sum(p_, axis=-1, keepdims=True) + jnp.exp(sk - m)
        o = jnp.dot(p_.astype(BF16), vd, preferred_element_type=F32) / den
        for p in range(group // 2):
            j = g * (group // 2) + p
            o2 = jnp.where(lo, o[2 * p * w:(2 * p + 1) * w], o[(2 * p + 1) * w:(2 * p + 2) * w])
            o_ref[:, j * LANES:(j + 1) * LANES] = o2.astype(BF16)


def _attention(q, k, v, sinks, batch, seq):
    n, dq = q.shape
    dkv = k.shape[1]
    nb = seq // WINDOW
    cur = lambda b, i: (b * nb + i, 0)
    prev = lambda b, i: (b * nb + jnp.maximum(i - 1, 0), 0)
    return pl.pallas_call(
        _attn_kernel,
        grid=(batch, nb),
        in_specs=[
            pl.BlockSpec(memory_space=pltpu.SMEM),
            pl.BlockSpec((WINDOW, dq), cur),
            pl.BlockSpec((WINDOW, dkv), prev), pl.BlockSpec((WINDOW, dkv), cur),
            pl.BlockSpec((WINDOW, dkv), prev), pl.BlockSpec((WINDOW, dkv), cur),
        ],
        out_specs=pl.BlockSpec((WINDOW, dq), cur),
        out_shape=jax.ShapeDtypeStruct((n, dq), BF16),
        compiler_params=_cparams(("parallel", "parallel")),
        name="swa_attention",
    )(sinks, q, k, k, v, v)


def _proj_kernel(a_ref, w_ref, res_ref, g_ref, h_ref, hn_ref, *, tiled_rows):
    h = res_ref[...] + jnp.dot(a_ref[...], w_ref[...], preferred_element_type=F32)
    h_ref[...] = h
    hn = _rms(h, g_ref[...])
    if tiled_rows:
        tm = h.shape[0]
        for s in range(h.shape[1] // LANES):
            hn_ref[pl.ds(s, tm, stride=SUBLANES), :] = hn[:, s * LANES:(s + 1) * LANES]
    else:
        hn_ref[...] = hn.astype(hn_ref.dtype)


def _proj_res_norm(a, w, res, g, *, tiled_rows):
    n, kdim = a.shape
    d = w.shape[1]
    tm = ROW_TILE
    row = lambda i: (i, 0)
    const = lambda i: (0, 0)
    if tiled_rows:
        assert d == SUBLANES * LANES
        hn_spec = pl.BlockSpec((tm * SUBLANES, LANES), row)
        hn_shape = jax.ShapeDtypeStruct((n * SUBLANES, LANES), F32)
    else:
        hn_spec = pl.BlockSpec((tm, d), row)
        hn_shape = jax.ShapeDtypeStruct((n, d), BF16)
    return pl.pallas_call(
        functools.partial(_proj_kernel, tiled_rows=tiled_rows),
        grid=(n // tm,),
        in_specs=[pl.BlockSpec((tm, kdim), row), pl.BlockSpec((kdim, d), const),
                  pl.BlockSpec((tm, d), row), pl.BlockSpec((1, d), const)],
        out_specs=[pl.BlockSpec((tm, d), row), hn_spec],
        out_shape=[jax.ShapeDtypeStruct((n, d), F32), hn_shape],
        compiler_params=_cparams(("parallel",)),
        name="proj_res_norm",
    )(a, w, res, g)


def _swiglu_partial(xb, wg, wu, wd):
    hg = jnp.dot(xb, wg, preferred_element_type=F32)
    hu = jnp.dot(xb, wu, preferred_element_type=F32)
    act = (_silu(hg) * hu).astype(BF16)
    return jnp.dot(act, wd, preferred_element_type=F32)


def _ffn_dense_kernel(x_ref, wg_ref, wu_ref, wd_ref, res_ref, g_ref, h_ref, hn_ref, acc_ref):
    j = pl.program_id(1)

    @pl.when(j == 0)
    def _():
        acc_ref[...] = res_ref[...]

    acc_ref[...] += _swiglu_partial(x_ref[...], wg_ref[...], wu_ref[...], wd_ref[...])

    @pl.when(j == pl.num_programs(1) - 1)
    def _():
        h = acc_ref[...]
        h_ref[...] = h
        hn_ref[...] = _rms(h, g_ref[...]).astype(hn_ref.dtype)


def _ffn_dense(xn, wg, wu, wd, res, g):
    n, d = xn.shape
    ff = wg.shape[1]
    tm, tf = ROW_TILE, FF_TILE
    row = lambda i, j: (i, 0)
    return pl.pallas_call(
        _ffn_dense_kernel,
        grid=(n // tm, ff // tf),
        in_specs=[
            pl.BlockSpec((tm, d), row),
            pl.BlockSpec((d, tf), lambda i, j: (0, j)),
            pl.BlockSpec((d, tf), lambda i, j: (0, j)),
            pl.BlockSpec((tf, d), lambda i, j: (j, 0)),
            pl.BlockSpec((tm, d), row),
            pl.BlockSpec((1, d), lambda i, j: (0, 0)),
        ],
        out_specs=[pl.BlockSpec((tm, d), row), pl.BlockSpec((tm, d), row)],
        out_shape=[jax.ShapeDtypeStruct((n, d), F32), jax.ShapeDtypeStruct((n, d), BF16)],
        scratch_shapes=[pltpu.VMEM((tm, d), F32)],
        compiler_params=_cparams(("parallel", "arbitrary")),
        name="swiglu_dense",
    )(xn, wg, wu, wd, res, g)


def _ffn_moe_kernel(vt_ref, ve_ref, vlo_ref, vhi_ref, vfirst_ref, nv_ref,
                    x_ref, wg_ref, wu_ref, wd_ref, y_ref, acc_ref, xb_ref):
    v = pl.program_id(0)
    j = pl.program_id(1)
    tm, d = acc_ref.shape
    nslab = d // LANES

    @pl.when(v < nv_ref[0])
    def _():
        @pl.when(j == 0)
        def _():
            slabs = [x_ref[pl.ds(s, tm, stride=SUBLANES), :] for s in range(nslab)]
            xb_ref[...] = jnp.concatenate(slabs, axis=-1).astype(BF16)
            acc_ref[...] = jnp.zeros_like(acc_ref)

        acc_ref[...] += _swiglu_partial(xb_ref[...], wg_ref[...], wu_ref[...], wd_ref[...])

        @pl.when(j == pl.num_programs(1) - 1)
        def _():
            r = vt_ref[v] * tm + lax.broadcasted_iota(I32, (tm, 1), 0)
            mine = (r >= vlo_ref[v]) & (r < vhi_ref[v])

            @pl.when(vfirst_ref[v] == 1)
            def _():
                for s in range(nslab):
                    new = acc_ref[:, s * LANES:(s + 1) * LANES]
                    y_ref[pl.ds(s, tm, stride=SUBLANES), :] = jnp.where(mine, new, 0.0)

            @pl.when(vfirst_ref[v] != 1)
            def _():
                for s in range(nslab):
                    rows = pl.ds(s, tm, stride=SUBLANES)
                    new = acc_ref[:, s * LANES:(s + 1) * LANES]
                    y_ref[rows, :] = jnp.where(mine, new, y_ref[rows, :])


def _ffn_moe(xs_t, wg, wu, wd, visits, n_visits_max):
    vt, ve, vlo, vhi, vfirst, nv = visits
    rows8, _ = xs_t.shape
    p = rows8 // SUBLANES
    d = wg.shape[1]
    ff = wg.shape[2]
    tm, tf = ROW_TILE, FF_TILE
    nf = ff // tf

    def jj(v, j, nv_ref):
        return jnp.where(v < nv_ref[0], j, nf - 1)

    xmap = lambda v, j, vt, ve, vlo, vhi, vf, nv: (vt[v], 0)
    gmap = lambda v, j, vt, ve, vlo, vhi, vf, nv: (ve[v], 0, jj(v, j, nv))
    dmap = lambda v, j, vt, ve, vlo, vhi, vf, nv: (ve[v], jj(v, j, nv), 0)
    grid_spec = pltpu.PrefetchScalarGridSpec(
        num_scalar_prefetch=6,
        grid=(n_visits_max, nf),
        in_specs=[
            pl.BlockSpec((tm * SUBLANES, LANES), xmap),
            pl.BlockSpec((None, d, tf), gmap),
            pl.BlockSpec((None, d, tf), gmap),
            pl.BlockSpec((None, tf, d), dmap),
        ],
        out_specs=pl.BlockSpec((tm * SUBLANES, LANES), xmap),
        scratch_shapes=[pltpu.VMEM((tm, d), F32), pltpu.VMEM((tm, d), BF16)],
    )
    return pl.pallas_call(
        _ffn_moe_kernel,
        grid_spec=grid_spec,
        out_shape=jax.ShapeDtypeStruct((p * SUBLANES, LANES), F32),
        compiler_params=_cparams(("arbitrary", "arbitrary")),
        name="swiglu_moe",
    )(vt, ve, vlo, vhi, vfirst, nv, xs_t, wg, wu, wd)


def _mm_kernel(x_ref, w_ref, o_ref):
    o_ref[...] = jnp.dot(x_ref[...], w_ref[...], preferred_element_type=F32).astype(o_ref.dtype)


def _matmul(x, w, tm, tn, out_dtype):
    n, kdim = x.shape
    m = w.shape[1]
    return pl.pallas_call(
        _mm_kernel,
        grid=(m // tn, n // tm),
        in_specs=[pl.BlockSpec((tm, kdim), lambda c, r: (r, 0)), pl.BlockSpec((kdim, tn), lambda c, r: (0, c))],
        out_specs=pl.BlockSpec((tm, tn), lambda c, r: (r, c)),
        out_shape=jax.ShapeDtypeStruct((n, m), out_dtype),
        compiler_params=_cparams(("parallel", "parallel")),
        name="matmul",
    )(x, w)


def _gdn_kernel(alog_ref, dtb_ref, q_ref, k_ref, v_ref, z_ref, gt_ref, cwq_ref, cwk_ref, cwv_ref, on_ref,
                o_ref, u_ref, wq_ref, in_ref, kd_ref, gl_ref):
    h = pl.program_id(1)
    t, dh = q_ref.shape
    c = GDN_CHUNK
    grp = GDN_GROUP
    ngroups = t // (c * grp)
    halo = SUBLANES
    ii = lax.broadcasted_iota(I32, (c, c), 0)
    jx = lax.broadcasted_iota(I32, (c, c), 1)
    causal = ii >= jx
    strict = ii > jx
    bits = ii ^ jx
    eye = jnp.where(ii == jx, 1.0, 0.0)
    pos = lax.broadcasted_iota(I32, (c, dh), 0)
    nt = (((1,), (1,)), ((), ()))
    tn = (((0,), (0,)), ((), ()))
    neg_a = -jnp.exp(alog_ref[h])
    dtb = dtb_ref[h]

    def conv_silu(x_ref, w_ref, ci, rs, rp):
        x = x_ref[rs, :]
        prev = jnp.where(ci > 0, x_ref[rp, :], 0.0)
        xx = jnp.concatenate([prev, x], axis=0)
        w = w_ref[...]
        y = x * w[CONV_WIDTH - 1:CONV_WIDTH, :]
        for s in range(1, CONV_WIDTH):
            y = y + pltpu.roll(xx, s, 0)[halo:] * w[CONV_WIDTH - 1 - s:CONV_WIDTH - s, :]
        return _silu(y)

    def l2n(x):
        return x * lax.rsqrt(jnp.sum(x * x, axis=-1, keepdims=True) + EPS)

    def local_terms(ci):
        r0 = pl.multiple_of(ci * c, c)
        rs = pl.ds(r0, c)
        rp = pl.ds(pl.multiple_of(jnp.maximum(r0 - halo, 0), halo), halo)
        qc = l2n(conv_silu(q_ref, cwq_ref, ci, rs, rp)) * (dh ** -0.5)
        kc = l2n(conv_silu(k_ref, cwk_ref, ci, rs, rp))
        vc = conv_silu(v_ref, cwv_ref, ci, rs, rp)
        gt = gt_ref[rs, :]
        lane = lax.broadcasted_iota(I32, gt.shape, 1)
        b_log = jnp.sum(jnp.where(lane == h, gt, 0.0), axis=-1, keepdims=True)
        a_log = jnp.sum(jnp.where(lane == h + B_HEADS, gt, 0.0), axis=-1, keepdims=True)
        beta = jnp.broadcast_to(1.0 / (1.0 + jnp.exp(-b_log)), (c, dh))
        xa = a_log + dtb
        softplus = jnp.maximum(xa, 0.0) + jnp.log(1.0 + jnp.exp(-jnp.abs(xa)))
        gcb = jnp.broadcast_to(neg_a * softplus, (c, dh))
        sh = 1
        while sh < c:
            gcb = gcb + jnp.where(pos >= sh, pltpu.roll(gcb, sh, 0), 0.0)
            sh *= 2
        return qc, kc, vc, beta, gcb

    def prepare(cis):
        terms = [local_terms(ci) for ci in cis]
        yield
        lowers, intras, kbs, egs = [], [], [], []
        for qc, kc, vc, beta, gcb in terms:
            diff = gcb - gcb.T
            decay = jnp.exp(jnp.where(causal, diff, NEG))
            kb = kc * beta
            both = lax.dot_general(jnp.concatenate([kb.astype(BF16), qc.astype(BF16)], axis=0),
                                   kc.astype(BF16), nt, preferred_element_type=F32)
            lowers.append(jnp.where(strict, both[:c] * decay, 0.0))
            intras.append(jnp.where(causal, both[c:] * decay, 0.0))
            kbs.append(kb)
            egs.append(jnp.exp(gcb))
        yield
        tinvs = [eye - jnp.where(bits < 2, lower, 0.0) for lower in lowers]
        m = 2
        while m < c:
            level = (bits >= m) & (bits < 2 * m)
            tbs = [tinv.astype(BF16) for tinv in tinvs]
            ys = [jnp.dot(jnp.where(level, lower, 0.0).astype(BF16), tb, preferred_element_type=F32).astype(BF16)
                  for lower, tb in zip(lowers, tbs)]
            yield
            tinvs = [tinv - jnp.dot(tb, y, preferred_element_type=F32) for tinv, tb, y in zip(tinvs, tbs, ys)]
            yield
            m *= 2
        for ci, (qc, kc, vc, beta, gcb), tinv, intra, kb, eg in zip(cis, terms, tinvs, intras, kbs, egs):
            r0 = pl.multiple_of(ci * c, c)
            rs = pl.ds(r0, c)
            rhs = jnp.concatenate([vc * beta, kb * eg], axis=-1)
            x = rhs + jnp.dot((tinv - eye).astype(BF16), rhs.astype(BF16), preferred_element_type=F32)
            glast = gcb[c - 1:c, :]
            u_ref[rs, :] = x[:, :dh]
            wq_ref[pl.ds(pl.multiple_of(2 * r0, c), c), :] = x[:, dh:].astype(BF16)
            wq_ref[pl.ds(pl.multiple_of(2 * r0 + c, c), c), :] = (qc * eg).astype(BF16)
            in_ref[rs, :] = intra.astype(BF16)
            kd_ref[rs, :] = (kc * jnp.exp(glast - gcb)).astype(BF16)
            gl_ref[pl.ds(pl.multiple_of(ci * halo, halo), halo), :] = jnp.broadcast_to(jnp.exp(glast), (halo, dh))
        yield

    def scan(ci, state):
        r0 = pl.multiple_of(ci * c, c)
        rs = pl.ds(r0, c)
        xs = jnp.dot(wq_ref[pl.ds(pl.multiple_of(2 * r0, c), 2 * c), :], state.astype(BF16),
                     preferred_element_type=F32)
        vnew = (u_ref[rs, :] - xs[:c]).astype(BF16)
        o = xs[c:] + jnp.dot(in_ref[rs, :], vnew, preferred_element_type=F32)
        dec = gl_ref[pl.ds(pl.multiple_of(ci * halo, halo), 1), :]
        state = state * dec + lax.dot_general(kd_ref[rs, :], vnew, tn, preferred_element_type=F32)
        on = _rms(o, on_ref[...]) * _silu(z_ref[rs, :])
        o_ref[rs, :] = on.astype(o_ref.dtype)
        return state

    for _ in prepare([jnp.int32(k) for k in range(grp)]):
        pass

    def body(g, state):
        stages = prepare([g * grp + k for k in range(grp)])
        n_stages = 2 * (c.bit_length() - 2) + 3
        done = 0
        for k in range(grp):
            while done < (k * n_stages) // grp:
                next(stages)
                done += 1
            state = scan((g - 1) * grp + k, state)
        for _ in stages:
            pass
        return state

    state = lax.fori_loop(1, ngroups, body, jnp.zeros((dh, dh), F32))
    for k in range(grp):
        state = scan(jnp.int32((ngroups - 1) * grp + k), state)


def _gated_deltanet(qkvz, gates, conv_w, a_log, dt_bias, o_norm, batch, seq):
    n = qkvz.shape[0]
    dh = B_HEAD_DIM
    nh = B_HEADS
    c = GDN_CHUNK
    assert seq % (c * GDN_GROUP) == 0
    blk = lambda off: pl.BlockSpec((seq, dh), lambda b, h, off=off: (b, off + h))
    cw = lambda off: pl.BlockSpec((CONV_WIDTH, dh), lambda b, h, off=off: (0, off + h))
    smem = pl.BlockSpec(memory_space=pltpu.SMEM)
    scratch = [pltpu.VMEM((seq, dh), F32), pltpu.VMEM((2 * seq, dh), BF16), pltpu.VMEM((seq, c), BF16),
               pltpu.VMEM((seq, dh), BF16), pltpu.VMEM((seq // c * SUBLANES, dh), F32)]
    return pl.pallas_call(
        _gdn_kernel,
        grid=(batch, nh),
        in_specs=[smem, smem, blk(0), blk(nh), blk(2 * nh), blk(3 * nh),
                  pl.BlockSpec((seq, LANES), lambda b, h: (b, 0)),
                  cw(0), cw(nh), cw(2 * nh),
                  pl.BlockSpec((1, dh), lambda b, h: (0, 0))],
        out_specs=pl.BlockSpec((seq, dh), lambda b, h: (b, h)),
        out_shape=jax.ShapeDtypeStruct((n, nh * dh), BF16),
        scratch_shapes=scratch,
        compiler_params=_cparams(("parallel", "parallel")),
        name="gated_deltanet",
    )(a_log, dt_bias, qkvz, qkvz, qkvz, qkvz, gates, conv_w, conv_w, conv_w, o_norm)


def _route_kernel(hn_ref, wr_ref, e1_ref, e2_ref, w1_ref, w2_ref, r1_ref, r2_ref, cnt_ref, base_ref):
    i = pl.program_id(0)
    tm, d = hn_ref.shape[0] // SUBLANES, SUBLANES * LANES
    ne = wr_ref.shape[0]

    @pl.when(i == 0)
    def _():
        base_ref[...] = jnp.zeros_like(base_ref)

    lg = jnp.zeros((ne, tm), F32)
    for s in range(SUBLANES):
        lg = lg + lax.dot_general(wr_ref[:, s * LANES:(s + 1) * LANES], hn_ref[pl.ds(s, tm, stride=SUBLANES), :],
                                  (((1,), (1,)), ((), ())), precision=lax.Precision.HIGHEST,
                                  preferred_element_type=F32)
    eio = lax.broadcasted_iota(I32, (ne, tm), 0)
    m1 = jnp.max(lg, axis=0, keepdims=True)
    i1 = jnp.min(jnp.where(lg == m1, eio, ne), axis=0, keepdims=True)
    lg2 = jnp.where(eio == i1, -jnp.inf, lg)
    m2 = jnp.max(lg2, axis=0, keepdims=True)
    i2 = jnp.min(jnp.where(lg2 == m2, eio, ne), axis=0, keepdims=True)
    ex = jnp.exp(m2 - m1)
    w1 = 1.0 / (1.0 + ex)
    sel = jnp.where((eio == i1) | (eio == i2), 1.0, 0.0)
    tri = jnp.where(lax.broadcasted_iota(I32, (tm, tm), 0) < lax.broadcasted_iota(I32, (tm, tm), 1), 1.0, 0.0)
    pre = jnp.dot(sel.astype(BF16), tri.astype(BF16), preferred_element_type=F32) + base_ref[:, 0:1]
    r1 = jnp.sum(jnp.where(eio == i1, pre, 0.0), axis=0, keepdims=True)
    r2 = jnp.sum(jnp.where(eio == i2, pre, 0.0), axis=0, keepdims=True)
    base_ref[...] += jnp.sum(sel, axis=1, keepdims=True)
    e1_ref[0] = i1
    e2_ref[0] = i2
    w1_ref[0] = w1
    w2_ref[0] = ex * w1
    r1_ref[0] = r1.astype(I32)
    r2_ref[0] = r2.astype(I32)
    cnt_ref[...] = base_ref[...]


def _route(hn_t, wr_t):
    n = hn_t.shape[0] // SUBLANES
    ne = wr_t.shape[0]
    tm = ROW_TILE
    nt = n // tm
    vec = pl.BlockSpec((1, 1, tm), lambda i: (i, 0, 0))
    ish = jax.ShapeDtypeStruct((nt, 1, tm), I32)
    fsh = jax.ShapeDtypeStruct((nt, 1, tm), F32)
    return pl.pallas_call(
        _route_kernel,
        grid=(nt,),
        in_specs=[pl.BlockSpec((tm * SUBLANES, LANES), lambda i: (i, 0)),
                  pl.BlockSpec((ne, SUBLANES * LANES), lambda i: (0, 0))],
        out_specs=[vec, vec, vec, vec, vec, vec, pl.BlockSpec((ne, LANES), lambda i: (0, 0))],
        out_shape=[ish, ish, fsh, fsh, ish, ish, jax.ShapeDtypeStruct((ne, LANES), F32)],
        scratch_shapes=[pltpu.VMEM((ne, LANES), F32)],
        compiler_params=_cparams(("arbitrary",)),
        name="moe_route",
    )(hn_t, wr_t)


def _dispatch_kernel(p1_ref, p2_ref, hn_ref, xs_ref, sem):
    i = pl.program_id(0)
    tg = hn_ref.shape[0] // SUBLANES

    def tile_copy(k, pos):
        src = hn_ref.at[pl.ds(pl.multiple_of(k * SUBLANES, SUBLANES), SUBLANES), :]
        dst = xs_ref.at[pl.ds(pl.multiple_of(pos * SUBLANES, SUBLANES), SUBLANES), :]
        return pltpu.make_async_copy(src, dst, sem)

    def issue(k, carry):
        t = i * tg + k
        tile_copy(k, p1_ref[t]).start()
        tile_copy(k, p2_ref[t]).start()
        return carry

    lax.fori_loop(0, tg, issue, 0, unroll=8)

    def drain(k, carry):
        tile_copy(0, 0).wait()
        tile_copy(0, 0).wait()
        return carry

    lax.fori_loop(0, tg, drain, 0, unroll=8)


def _dispatch(pos1, pos2, hn_t):
    rows8 = hn_t.shape[0]
    n = rows8 // SUBLANES
    tg = ROW_TILE
    grid_spec = pltpu.PrefetchScalarGridSpec(
        num_scalar_prefetch=2,
        grid=(n // tg,),
        in_specs=[pl.BlockSpec((tg * SUBLANES, LANES), lambda i, p1, p2: (i, 0))],
        out_specs=pl.BlockSpec(memory_space=pl.ANY),
        scratch_shapes=[pltpu.SemaphoreType.DMA(())],
    )
    return pl.pallas_call(
        _dispatch_kernel,
        grid_spec=grid_spec,
        out_shape=jax.ShapeDtypeStruct((2 * rows8, LANES), F32),
        compiler_params=_cparams(("arbitrary",)),
        name="moe_dispatch",
    )(pos1, pos2, hn_t)


def _combine_kernel(p1_ref, p2_ref, y_ref, h_ref, w1_ref, w2_ref, g_ref, o_ref, y1_ref, y2_ref, sem):
    i = pl.program_id(0)
    tc, d = h_ref.shape
    slot = i % 2

    def tile_copy(pos, buf, sl, k):
        src = y_ref.at[pl.ds(pl.multiple_of(pos * SUBLANES, SUBLANES), SUBLANES), :]
        dst = buf.at[sl, pl.ds(pl.multiple_of(k * SUBLANES, SUBLANES), SUBLANES), :]
        return pltpu.make_async_copy(src, dst, sem.at[sl])

    def start_gathers(step, sl):
        def issue(k, carry):
            t = step * tc + k
            tile_copy(p1_ref[t], y1_ref, sl, k).start()
            tile_copy(p2_ref[t], y2_ref, sl, k).start()
            return carry

        lax.fori_loop(0, tc, issue, 0, unroll=8)

    @pl.when(i == 0)
    def _():
        start_gathers(0, 0)

    @pl.when(i + 1 < pl.num_programs(0))
    def _():
        start_gathers(i + 1, 1 - slot)

    def drain(k, carry):
        tile_copy(0, y1_ref, slot, 0).wait()
        tile_copy(0, y2_ref, slot, 0).wait()
        return carry

    lax.fori_loop(0, tc, drain, 0, unroll=8)

    w1 = w1_ref[...]
    w2 = w2_ref[...]
    slabs = []
    ss = jnp.zeros((tc, 1), F32)
    for s in range(d // LANES):
        rows = pl.ds(s, tc, stride=SUBLANES)
        hs = h_ref[:, s * LANES:(s + 1) * LANES] + w1 * y1_ref[slot, rows, :] + w2 * y2_ref[slot, rows, :]
        ss = ss + jnp.sum(hs * hs, axis=-1, keepdims=True)
        slabs.append(hs)
    inv = lax.rsqrt(ss / d + EPS)
    for s, hs in enumerate(slabs):
        o_ref[:, s * LANES:(s + 1) * LANES] = hs * inv * g_ref[:, s * LANES:(s + 1) * LANES]


def _combine(pos1, pos2, y_t, h, w1, w2, g):
    n, d = h.shape
    tc = COMBINE_TILE
    row = lambda i, p1, p2: (i, 0)
    grid_spec = pltpu.PrefetchScalarGridSpec(
        num_scalar_prefetch=2,
        grid=(n // tc,),
        in_specs=[pl.BlockSpec(memory_space=pl.ANY),
                  pl.BlockSpec((tc, d), row),
                  pl.BlockSpec((tc, 1), row), pl.BlockSpec((tc, 1), row),
                  pl.BlockSpec((1, d), lambda i, p1, p2: (0, 0))],
        out_specs=pl.BlockSpec((tc, d), row),
        scratch_shapes=[pltpu.VMEM((2, tc * SUBLANES, LANES), F32), pltpu.VMEM((2, tc * SUBLANES, LANES), F32),
                        pltpu.SemaphoreType.DMA((2,))],
    )
    return pl.pallas_call(
        _combine_kernel,
        grid_spec=grid_spec,
        out_shape=jax.ShapeDtypeStruct((n, d), F32),
        compiler_params=_cparams(("arbitrary",)),
        name="moe_combine",
    )(pos1, pos2, y_t, h, w1, w2, g)


def _visit_plan(counts, n_slots):
    ne = counts.shape[0]
    tm = ROW_TILE
    n_tiles = n_slots // tm
    n_visits_max = n_tiles + ne - 1
    ends = jnp.cumsum(counts)
    starts = ends - counts
    t0 = jnp.arange(n_tiles, dtype=I32)[:, None] * tm
    lo = jnp.maximum(starts[None, :], t0)
    hi = jnp.minimum(ends[None, :], t0 + tm)
    hit = (hi > lo).reshape(-1)
    nv = jnp.sum(hit).astype(I32)
    idx = jnp.nonzero(hit, size=n_visits_max, fill_value=0)[0].astype(I32)
    last = idx[jnp.maximum(nv - 1, 0)]
    idx = jnp.where(jnp.arange(n_visits_max) < nv, idx, last)
    vt = idx // ne
    ve = idx % ne
    vlo = starts[ve]
    vhi = ends[ve]
    vfirst = jnp.concatenate([jnp.ones((1,), I32), (vt[1:] != vt[:-1]).astype(I32)])
    return (vt, ve, vlo.astype(I32), vhi.astype(I32), vfirst, nv.reshape(1)), n_visits_max


def _rope_tables(seq):
    half = A_HEAD_DIM // 2
    inv = 1.0 / (ROPE_THETA ** (jnp.arange(0, A_HEAD_DIM, 2, dtype=F32) / A_HEAD_DIM))
    ang = jnp.arange(seq, dtype=F32)[:, None] * inv[None, :]
    cos, sin = jnp.cos(ang), jnp.sin(ang)
    reps = LANES // A_HEAD_DIM
    cos_t = jnp.tile(jnp.concatenate([cos, cos], axis=-1), (1, reps))
    sin_t = jnp.tile(jnp.concatenate([-sin, sin], axis=-1), (1, reps))
    return cos_t, sin_t


def kernel(x, a_norm, a_w_qkv, a_b_qkv, a_sinks, a_w_o, f_norm, f_w_gate, f_w_up, f_w_down,
           b_norm, b_w_in, b_conv_w, b_a_log, b_dt_bias, b_o_norm, b_w_o,
           m_norm, m_w_router, m_w_gate, m_w_up, m_w_down, final_norm):
    batch, seq, d = x.shape
    n = batch * seq
    x2 = x.reshape(n, d)
    row = lambda v: v.reshape(1, -1).astype(F32)

    nq = A_HEADS * A_HEAD_DIM
    nkv = A_KV_HEADS * A_HEAD_DIM
    dup = jnp.repeat(jnp.arange(A_KV_HEADS), 2)[:, None] * A_HEAD_DIM + jnp.arange(A_HEAD_DIM)[None, :]
    cols = jnp.concatenate([jnp.arange(nq), nq + dup.reshape(-1), nq + nkv + dup.reshape(-1)])
    w_qkv = a_w_qkv[0][:, cols].astype(BF16)
    b_qkv = a_b_qkv[0][cols].reshape(1, -1)
    cos_t, sin_t = _rope_tables(seq)
    scale = A_HEAD_DIM ** -0.5
    q, k, v = _qkv_proj(x2, row(a_norm[0]), w_qkv, b_qkv, cos_t * scale, sin_t * scale, cos_t, sin_t, seq)
    att = _attention(q, k, v, a_sinks[0].astype(F32), batch, seq)
    h, hn = _proj_res_norm(att, a_w_o[0].astype(BF16), x2, row(f_norm[0]), tiled_rows=False)
    h, hn = _ffn_dense(hn, f_w_gate[0].astype(BF16), f_w_up[0].astype(BF16), f_w_down[0].astype(BF16),
                       h, row(b_norm[0]))

    nqkvz = 4 * B_HEADS * B_HEAD_DIM
    w_in = b_w_in[0]
    qkvz = _matmul(hn, w_in[:, :nqkvz].astype(BF16), 2 * ROW_TILE, B_HEADS * B_HEAD_DIM, F32)
    w_gates = jnp.pad(w_in[:, nqkvz:], ((0, 0), (0, LANES - 2 * B_HEADS))).astype(BF16)
    gates = _matmul(hn, w_gates, 2 * ROW_TILE, LANES, F32)
    gdn = _gated_deltanet(qkvz, gates, b_conv_w[0], b_a_log[0].astype(F32), b_dt_bias[0].astype(F32),
                          row(b_o_norm[0]), batch, seq)
    h, hn_t = _proj_res_norm(gdn, b_w_o[0].astype(BF16), h, row(m_norm[0]), tiled_rows=True)

    e1, e2, w1, w2, r1, r2, cnt = _route(hn_t, m_w_router[0].T.astype(F32))
    counts = cnt[:, 0].astype(I32)
    starts = jnp.cumsum(counts) - counts
    e1, e2, r1, r2 = (a.reshape(n) for a in (e1, e2, r1, r2))
    pos1 = starts[e1] + r1
    pos2 = starts[e2] + r2
    visits, n_visits_max = _visit_plan(counts, 2 * n)
    xs_t = _dispatch(pos1, pos2, hn_t)
    y_t = _ffn_moe(xs_t, m_w_gate[0].astype(BF16), m_w_up[0].astype(BF16), m_w_down[0].astype(BF16),
                   visits, n_visits_max)
    out = _combine(pos1, pos2, y_t, h, w1.reshape(n, 1), w2.reshape(n, 1), row(final_norm))
    return out.reshape(batch, seq, d)
```

```python
import functools
import math

import jax
import jax.numpy as jnp
from jax import lax
from jax.experimental import pallas as pl
from jax.experimental.pallas import tpu as pltpu

F32 = jnp.float32
BF16 = jnp.bfloat16
I32 = jnp.int32

EPS = 1e-6
LANES = 128
SUBLANES = 8
VMEM_LIMIT = 60 * 1024 * 1024

A_HEADS, A_KV_HEADS, A_HEAD_DIM = 16, 4, 64
WINDOW = 128
ROPE_THETA = 10000.0
B_HEADS, B_HEAD_DIM = 8, 128
CONV_WIDTH = 4
N_EXPERTS = 8
GDN_CHUNK = 128
GDN_GROUP = 8
GDN_RING = 3
GDN_HEADS = 4
NEG = -1e30

ROW_TILE = 512
FF_TILE = 1792
COMBINE_TILE = 256


def _cparams(sem):
    return pltpu.CompilerParams(dimension_semantics=sem, vmem_limit_bytes=VMEM_LIMIT)


def _rms(x, g):
    var = jnp.mean(x * x, axis=-1, keepdims=True)
    return x * lax.rsqrt(var + EPS) * g


def _silu(x):
    return x * (1.0 / (1.0 + jnp.exp(-x)))


def _qkv_kernel(x_ref, g_ref, w_ref, b_ref, cq_ref, sq_ref, ck_ref, sk_ref, q_ref, k_ref, v_ref):
    tm = x_ref.shape[0]
    xn = _rms(x_ref[...], g_ref[...]).astype(BF16)
    lane = lax.broadcasted_iota(I32, (tm, LANES), 1)
    first_half = (lane % A_HEAD_DIM) < (A_HEAD_DIM // 2)

    def rope(xs, c, s):
        sw = jnp.where(first_half, pltpu.roll(xs, LANES - 32, 1), pltpu.roll(xs, 32, 1))
        return xs * c + sw * s

    nq = q_ref.shape[1] // LANES
    nk = k_ref.shape[1] // LANES
    nv = v_ref.shape[1] // LANES
    for s in range(0, nq + nk + nv, 2):
        acc = jnp.dot(xn, w_ref[:, s * LANES:(s + 2) * LANES], preferred_element_type=F32)
        acc = acc + b_ref[:, s * LANES:(s + 2) * LANES]
        for t in range(2):
            col = s + t
            part = acc[:, t * LANES:(t + 1) * LANES]
            if col < nq:
                q_ref[:, col * LANES:(col + 1) * LANES] = rope(part, cq_ref[...], sq_ref[...]).astype(BF16)
            elif col < nq + nk:
                c = col - nq
                k_ref[:, c * LANES:(c + 1) * LANES] = rope(part, ck_ref[...], sk_ref[...]).astype(BF16)
            else:
                c = col - nq - nk
                v_ref[:, c * LANES:(c + 1) * LANES] = part.astype(BF16)


def _qkv_proj(x2, g, w, b, cq, sq, ck, sk, seq):
    n, d = x2.shape
    tm = ROW_TILE
    nq = A_HEADS * A_HEAD_DIM
    nkv = 2 * A_KV_HEADS * A_HEAD_DIM
    tblk = seq // tm
    row = lambda i: (i, 0)
    const = lambda i: (0, 0)
    tab = lambda i: (i % tblk, 0)
    return pl.pallas_call(
        _qkv_kernel,
        grid=(n // tm,),
        in_specs=[
            pl.BlockSpec((tm, d), row),
            pl.BlockSpec((1, d), const),
            pl.BlockSpec((d, nq + 2 * nkv), const),
            pl.BlockSpec((1, nq + 2 * nkv), const),
            pl.BlockSpec((tm, LANES), tab), pl.BlockSpec((tm, LANES), tab),
            pl.BlockSpec((tm, LANES), tab), pl.BlockSpec((tm, LANES), tab),
        ],
        out_specs=[pl.BlockSpec((tm, nq), row), pl.BlockSpec((tm, nkv), row), pl.BlockSpec((tm, nkv), row)],
        out_shape=[jax.ShapeDtypeStruct((n, nq), BF16), jax.ShapeDtypeStruct((n, nkv), BF16),
                   jax.ShapeDtypeStruct((n, nkv), BF16)],
        compiler_params=_cparams(("parallel",)),
        name="qkv_rope",
    )(x2, g, w, b, cq, sq, ck, sk)


def _attn_kernel(sink_ref, q_ref, kp_ref, kc_ref, vp_ref, vc_ref, o_ref):
    n = pl.program_id(1)
    w = WINDOW
    group = A_HEADS // A_KV_HEADS
    rows = group * w
    lo = lax.broadcasted_iota(I32, (w, LANES), 1) < A_HEAD_DIM
    qi = lax.broadcasted_iota(I32, (rows, 2 * w), 0) % w
    kj = lax.broadcasted_iota(I32, (rows, 2 * w), 1)
    valid = (kj > qi) & (kj <= qi + w) & ((n > 0) | (kj >= w))
    hrow = lax.broadcasted_iota(I32, (rows, 1), 0) // w
    zero = jnp.zeros((w, LANES), BF16)
    for g in range(A_KV_HEADS):
        cs = slice(g * LANES, (g + 1) * LANES)
        kd = jnp.concatenate([kp_ref[:, cs], kc_ref[:, cs]], axis=0)
        vd = jnp.concatenate([vp_ref[:, cs], vc_ref[:, cs]], axis=0)
        parts = []
        for p in range(group // 2):
            j = g * (group // 2) + p
            q2 = q_ref[:, j * LANES:(j + 1) * LANES]
            parts.append(jnp.where(lo, q2, zero))
            parts.append(jnp.where(lo, zero, q2))
        qs = jnp.concatenate(parts, axis=0)
        s = lax.dot_general(qs, kd, (((1,), (1,)), ((), ())), preferred_element_type=F32)
        s = jnp.where(valid, s, NEG)
        sk = jnp.full((rows, 1), sink_ref[g * group], F32)
        for h in range(1, group):
            sk = jnp.where(hrow == h, sink_ref[g * group + h], sk)
        m = jnp.maximum(jnp.max(s, axis=-1, keepdims=True), sk)
        p_ = jnp.exp(s - m)
        den = jnp.sum(p_, axis=-1, keepdims=True) + jnp.exp(sk - m)
        o = jnp.dot(p_.astype(BF16), vd, preferred_element_type=F32) / den
        for p in range(group // 2):
            j = g * (group // 2) + p
            o2 = jnp.where(lo, o[2 * p * w:(2 * p + 1) * w], o[(2 * p + 1) * w:(2 * p + 2) * w])
            o_ref[:, j * LANES:(j + 1) * LANES] = o2.astype(BF16)


def _attention(q, k, v, sinks, batch, seq):
    n, dq = q.shape
    dkv = k.shape[1]
    nb = seq // WINDOW
    cur = lambda b, i: (b * nb + i, 0)
    prev = lambda b, i: (b * nb + jnp.maximum(i - 1, 0), 0)
    return pl.pallas_call(
        _attn_kernel,
        grid=(batch, nb),
        in_specs=[
            pl.BlockSpec(memory_space=pltpu.SMEM),
            pl.BlockSpec((WINDOW, dq), cur),
            pl.BlockSpec((WINDOW, dkv), prev), pl.BlockSpec((WINDOW, dkv), cur),
            pl.BlockSpec((WINDOW, dkv), prev), pl.BlockSpec((WINDOW, dkv), cur),
        ],
        out_specs=pl.BlockSpec((WINDOW, dq), cur),
        out_shape=jax.ShapeDtypeStruct((n, dq), BF16),
        compiler_params=_cparams(("parallel", "parallel")),
        name="swa_attention",
    )(sinks, q, k, k, v, v)


def _proj_kernel(a_ref, w_ref, res_ref, g_ref, h_ref, hn_ref, *, tiled_rows):
    if len(a_ref.shape) == 3:
        a = jnp.concatenate([a_ref[j] for j in range(a_ref.shape[0])], axis=-1)
    else:
        a = a_ref[...]
    h = res_ref[...] + jnp.dot(a, w_ref[...], preferred_element_type=F32)
    h_ref[...] = h
    hn = _rms(h, g_ref[...])
    if tiled_rows:
        tm = h.shape[0]
        for s in range(h.shape[1] // LANES):
            hn_ref[pl.ds(s, tm, stride=SUBLANES), :] = hn[:, s * LANES:(s + 1) * LANES]
    else:
        hn_ref[...] = hn.astype(hn_ref.dtype)


def _proj_res_norm(a, w, res, g, *, tiled_rows):
    kdim, d = w.shape
    n = res.shape[0]
    tm = ROW_TILE
    row = lambda i: (i, 0)
    const = lambda i: (0, 0)
    if a.ndim == 3:
        a_spec = pl.BlockSpec((a.shape[0], tm, a.shape[2]), lambda i: (0, i, 0))
    else:
        a_spec = pl.BlockSpec((tm, kdim), row)
    if tiled_rows:
        assert d == SUBLANES * LANES
        hn_spec = pl.BlockSpec((tm * SUBLANES, LANES), row)
        hn_shape = jax.ShapeDtypeStruct((n * SUBLANES, LANES), F32)
    else:
        hn_spec = pl.BlockSpec((tm, d), row)
        hn_shape = jax.ShapeDtypeStruct((n, d), BF16)
    return pl.pallas_call(
        functools.partial(_proj_kernel, tiled_rows=tiled_rows),
        grid=(n // tm,),
        in_specs=[a_spec, pl.BlockSpec((kdim, d), const),
                  pl.BlockSpec((tm, d), row), pl.BlockSpec((1, d), const)],
        out_specs=[pl.BlockSpec((tm, d), row), hn_spec],
        out_shape=[jax.ShapeDtypeStruct((n, d), F32), hn_shape],
        compiler_params=_cparams(("parallel",)),
        name="proj_res_norm",
    )(a, w, res, g)


def _swiglu_partial(xb, wg, wu, wd):
    hg = jnp.dot(xb, wg, preferred_element_type=F32)
    hu = jnp.dot(xb, wu, preferred_element_type=F32)
    act = (_silu(hg) * hu).astype(BF16)
    return jnp.dot(act, wd, preferred_element_type=F32)


def _ffn_dense_kernel(x_ref, wg_ref, wu_ref, wd_ref, res_ref, g_ref, h_ref, hn_ref, acc_ref):
    j = pl.program_id(1)

    @pl.when(j == 0)
    def _():
        acc_ref[...] = res_ref[...]

    acc_ref[...] += _swiglu_partial(x_ref[...], wg_ref[...], wu_ref[...], wd_ref[...])

    @pl.when(j == pl.num_programs(1) - 1)
    def _():
        h = acc_ref[...]
        h_ref[...] = h
        hn_ref[...] = _rms(h, g_ref[...]).astype(hn_ref.dtype)


def _ffn_dense(xn, wg, wu, wd, res, g):
    n, d = xn.shape
    ff = wg.shape[1]
    tm, tf = ROW_TILE, FF_TILE
    row = lambda i, j: (i, 0)
    return pl.pallas_call(
        _ffn_dense_kernel,
        grid=(n // tm, ff // tf),
        in_specs=[
            pl.BlockSpec((tm, d), row),
            pl.BlockSpec((d, tf), lambda i, j: (0, j)),
            pl.BlockSpec((d, tf), lambda i, j: (0, j)),
            pl.BlockSpec((tf, d), lambda i, j: (j, 0)),
            pl.BlockSpec((tm, d), row),
            pl.BlockSpec((1, d), lambda i, j: (0, 0)),
        ],
        out_specs=[pl.BlockSpec((tm, d), row), pl.BlockSpec((tm, d), row)],
        out_shape=[jax.ShapeDtypeStruct((n, d), F32), jax.ShapeDtypeStruct((n, d), BF16)],
        scratch_shapes=[pltpu.VMEM((tm, d), F32)],
        compiler_params=_cparams(("parallel", "arbitrary")),
        name="swiglu_dense",
    )(xn, wg, wu, wd, res, g)


def _ffn_moe_kernel(vt_ref, ve_ref, vlo_ref, vhi_ref, vfirst_ref, nv_ref,
                    x_ref, wg_ref, wu_ref, wd_ref, y_ref, acc_ref, xb_ref):
    v = pl.program_id(0)
    j = pl.program_id(1)
    tm, d = acc_ref.shape
    nslab = d // LANES

    @pl.when(v < nv_ref[0])
    def _():
        @pl.when(j == 0)
        def _():
            slabs = [x_ref[pl.ds(s, tm, stride=SUBLANES), :] for s in range(nslab)]
            xb_ref[...] = jnp.concatenate(slabs, axis=-1).astype(BF16)
            acc_ref[...] = jnp.zeros_like(acc_ref)

        acc_ref[...] += _swiglu_partial(xb_ref[...], wg_ref[...], wu_ref[...], wd_ref[...])

        @pl.when(j == pl.num_programs(1) - 1)
        def _():
            r = vt_ref[v] * tm + lax.broadcasted_iota(I32, (tm, 1), 0)
            mine = (r >= vlo_ref[v]) & (r < vhi_ref[v])

            @pl.when(vfirst_ref[v] == 1)
            def _():
                for s in range(nslab):
                    new = acc_ref[:, s * LANES:(s + 1) * LANES]
                    y_ref[pl.ds(s, tm, stride=SUBLANES), :] = jnp.where(mine, new, 0.0)

            @pl.when(vfirst_ref[v] != 1)
            def _():
                for s in range(nslab):
                    rows = pl.ds(s, tm, stride=SUBLANES)
                    new = acc_ref[:, s * LANES:(s + 1) * LANES]
                    y_ref[rows, :] = jnp.where(mine, new, y_ref[rows, :])


def _ffn_moe(xs_t, wg, wu, wd, visits, n_visits_max):
    vt, ve, vlo, vhi, vfirst, nv = visits
    rows8, _ = xs_t.shape
    p = rows8 // SUBLANES
    d = wg.shape[1]
    ff = wg.shape[2]
    tm, tf = ROW_TILE, FF_TILE
    nf = ff // tf

    def jj(v, j, nv_ref):
        return jnp.where(v < nv_ref[0], j, nf - 1)

    xmap = lambda v, j, vt, ve, vlo, vhi, vf, nv: (vt[v], 0)
    gmap = lambda v, j, vt, ve, vlo, vhi, vf, nv: (ve[v], 0, jj(v, j, nv))
    dmap = lambda v, j, vt, ve, vlo, vhi, vf, nv: (ve[v], jj(v, j, nv), 0)
    grid_spec = pltpu.PrefetchScalarGridSpec(
        num_scalar_prefetch=6,
        grid=(n_visits_max, nf),
        in_specs=[
            pl.BlockSpec((tm * SUBLANES, LANES), xmap),
            pl.BlockSpec((None, d, tf), gmap),
            pl.BlockSpec((None, d, tf), gmap),
            pl.BlockSpec((None, tf, d), dmap),
        ],
        out_specs=pl.BlockSpec((tm * SUBLANES, LANES), xmap),
        scratch_shapes=[pltpu.VMEM((tm, d), F32), pltpu.VMEM((tm, d), BF16)],
    )
    return pl.pallas_call(
        _ffn_moe_kernel,
        grid_spec=grid_spec,
        out_shape=jax.ShapeDtypeStruct((p * SUBLANES, LANES), F32),
        compiler_params=_cparams(("arbitrary", "arbitrary")),
        name="swiglu_moe",
    )(vt, ve, vlo, vhi, vfirst, nv, xs_t, wg, wu, wd)


def _mm_kernel(x_ref, w_ref, o_ref):
    acc = jnp.dot(x_ref[...], w_ref[...], preferred_element_type=F32)
    if len(o_ref.shape) == 3:
        for j in range(o_ref.shape[0]):
            o_ref[j] = acc[:, j * LANES:(j + 1) * LANES].astype(o_ref.dtype)
    else:
        o_ref[...] = acc.astype(o_ref.dtype)


def _matmul(x, w, tm, tn, out_dtype, *, head_major=False):
    n, kdim = x.shape
    m = w.shape[1]
    if head_major:
        out_spec = pl.BlockSpec((tn // LANES, tm, LANES), lambda c, r: (c, r, 0))
        out_shape = jax.ShapeDtypeStruct((m // LANES, n, LANES), out_dtype)
    else:
        out_spec = pl.BlockSpec((tm, tn), lambda c, r: (r, c))
        out_shape = jax.ShapeDtypeStruct((n, m), out_dtype)
    return pl.pallas_call(
        _mm_kernel,
        grid=(m // tn, n // tm),
        in_specs=[pl.BlockSpec((tm, kdim), lambda c, r: (r, 0)), pl.BlockSpec((kdim, tn), lambda c, r: (0, c))],
        out_specs=out_spec,
        out_shape=out_shape,
        compiler_params=_cparams(("parallel", "parallel")),
        name="matmul",
    )(x, w)


def _gdn_kernel(alog_ref, dtb_ref, q_ref, k_ref, v_ref, z_ref, gt_ref, cwq_ref, cwk_ref, cwv_ref, on_ref,
                o_ref, lo_ref, rhs_ref, u_ref, wq_ref, in_ref, kd_ref, gl_ref):
    hb = pl.program_id(1)
    nh, t, dh = q_ref.shape
    c = GDN_CHUNK
    grp = GDN_GROUP
    gph = t // (c * grp)
    ngroups = nh * gph
    halo = SUBLANES
    ii = lax.broadcasted_iota(I32, (c, c), 0)
    jx = lax.broadcasted_iota(I32, (c, c), 1)
    causal = ii >= jx
    strict = ii > jx
    bits = ii ^ jx
    eye = jnp.where(ii == jx, 1.0, 0.0)
    pos = lax.broadcasted_iota(I32, (c, dh), 0)
    nt = (((1,), (1,)), ((), ()))
    tn = (((0,), (0,)), ((), ()))

    def slot_rows(tk, k, rows_per_chunk=c):
        base = ((tk % GDN_RING) * grp + k) * rows_per_chunk
        return pl.ds(pl.multiple_of(base, rows_per_chunk), rows_per_chunk)

    def conv_silu(x_ref, w_ref, hh, ci, rs, rp):
        x = x_ref[hh, rs, :]
        prev = jnp.where(ci > 0, x_ref[hh, rp, :], 0.0)
        xx = jnp.concatenate([prev, x], axis=0)
        w = w_ref[hh]
        y = x * w[CONV_WIDTH - 1:CONV_WIDTH, :]
        for s in range(1, CONV_WIDTH):
            y = y + pltpu.roll(xx, s, 0)[halo:] * w[CONV_WIDTH - 1 - s:CONV_WIDTH - s, :]
        return _silu(y)

    def l2n(x):
        return x * lax.rsqrt(jnp.sum(x * x, axis=-1, keepdims=True) + EPS)

    def local(tk, stores):
        hh = tk // gph
        head = hb * nh + hh
        neg_a = -jnp.exp(alog_ref[head])
        dtb = dtb_ref[head]
        for k in range(grp):
            ci = (tk % gph) * grp + k
            r0 = pl.multiple_of(ci * c, c)
            rs = pl.ds(r0, c)
            rp = pl.ds(pl.multiple_of(jnp.maximum(r0 - halo, 0), halo), halo)
            qc = l2n(conv_silu(q_ref, cwq_ref, hh, ci, rs, rp)) * (dh ** -0.5)
            kc = l2n(conv_silu(k_ref, cwk_ref, hh, ci, rs, rp))
            vc = conv_silu(v_ref, cwv_ref, hh, ci, rs, rp)
            gt = gt_ref[rs, :]
            lane = lax.broadcasted_iota(I32, gt.shape, 1)
            b_log = jnp.sum(jnp.where(lane == head, gt, 0.0), axis=-1, keepdims=True)
            a_log = jnp.sum(jnp.where(lane == head + B_HEADS, gt, 0.0), axis=-1, keepdims=True)
            beta = jnp.broadcast_to(1.0 / (1.0 + jnp.exp(-b_log)), (c, dh))
            xa = a_log + dtb
            softplus = jnp.maximum(xa, 0.0) + jnp.log(1.0 + jnp.exp(-jnp.abs(xa)))
            gcb = jnp.broadcast_to(neg_a * softplus, (c, dh))
            sh = 1
            while sh < c:
                gcb = gcb + jnp.where(pos >= sh, pltpu.roll(gcb, sh, 0), 0.0)
                sh *= 2
            diff = gcb - gcb.T
            decay = jnp.exp(jnp.where(causal, diff, NEG))
            kb = kc * beta
            both = lax.dot_general(jnp.concatenate([kb.astype(BF16), qc.astype(BF16)], axis=0),
                                   kc.astype(BF16), nt, preferred_element_type=F32)
            eg = jnp.exp(gcb)
            glast = gcb[c - 1:c, :]
            stores.append((lo_ref, slot_rows(tk, k), jnp.where(strict, both[:c] * decay, 0.0)))
            stores.append((rhs_ref, slot_rows(tk, k), jnp.concatenate([vc * beta, kb * eg], axis=-1)))
            stores.append((wq_ref, pl.ds(pl.multiple_of((((tk % GDN_RING) * grp + k) * 2 + 1) * c, c), c),
                           (qc * eg).astype(BF16)))
            stores.append((in_ref, slot_rows(tk, k), jnp.where(causal, both[c:] * decay, 0.0).astype(BF16)))
            stores.append((kd_ref, slot_rows(tk, k), (kc * jnp.exp(glast - gcb)).astype(BF16)))
            stores.append((gl_ref, slot_rows(tk, k, halo), jnp.broadcast_to(jnp.exp(glast), (halo, dh))))
            yield

    def solve(tk, stores):
        lowers = [lo_ref[slot_rows(tk, k), :] for k in range(grp)]
        tinvs = [eye - jnp.where(bits < 2, lower, 0.0) for lower in lowers]
        yield
        m = 2
        while m < c:
            level = (bits >= m) & (bits < 2 * m)
            tbs = [tinv.astype(BF16) for tinv in tinvs]
            ys = [jnp.dot(jnp.where(level, lower, 0.0).astype(BF16), tb, preferred_element_type=F32).astype(BF16)
                  for lower, tb in zip(lowers, tbs)]
            yield
            tinvs = [tinv - jnp.dot(tb, y, preferred_element_type=F32) for tinv, tb, y in zip(tinvs, tbs, ys)]
            yield
            m *= 2
        for k, tinv in enumerate(tinvs):
            rhs = rhs_ref[slot_rows(tk, k), :]
            x = rhs + jnp.dot((tinv - eye).astype(BF16), rhs.astype(BF16), preferred_element_type=F32)
            stores.append((u_ref, slot_rows(tk, k), x[:, :dh]))
            stores.append((wq_ref, pl.ds(pl.multiple_of(((tk % GDN_RING) * grp + k) * 2 * c, c), c),
                           x[:, dh:].astype(BF16)))
        yield

    def scan(tk, k, state):
        hh = tk // gph
        ci = (tk % gph) * grp + k
        rs = pl.ds(pl.multiple_of(ci * c, c), c)
        if k == 0:
            state = jnp.where(tk % gph == 0, 0.0, state)
        wq = wq_ref[pl.ds(pl.multiple_of(((tk % GDN_RING) * grp + k) * 2 * c, 2 * c), 2 * c), :]
        xs = jnp.dot(wq, state.astype(BF16), preferred_element_type=F32)
        vnew = (u_ref[slot_rows(tk, k), :] - xs[:c]).astype(BF16)
        o = xs[c:] + jnp.dot(in_ref[slot_rows(tk, k), :], vnew, preferred_element_type=F32)
        dec = gl_ref[pl.ds(pl.multiple_of(((tk % GDN_RING) * grp + k) * halo, halo), 1), :]
        state = state * dec + lax.dot_general(kd_ref[slot_rows(tk, k), :], vnew, tn, preferred_element_type=F32)
        on = _rms(o, on_ref[...]) * _silu(z_ref[hh, rs, :])
        o_ref[hh, rs, :] = on.astype(o_ref.dtype)
        return state

    n_solve_stages = 2 * (c.bit_length() - 2) + 2

    def tick(tk, state):
        static = isinstance(tk, int)
        do_local = (not static) or tk < ngroups
        do_solve = (not static) or 1 <= tk <= ngroups
        do_scan = (not static) or 2 <= tk <= ngroups + 1
        tk = jnp.int32(tk)
        stores = []
        solver = solve(tk - 1, stores) if do_solve else iter(())
        locals_ = local(tk, stores) if do_local else iter(())
        n_local = n_scan = 0
        for i in range(n_solve_stages):
            next(solver, None)
            while n_local < -((i + 1) * grp // -n_solve_stages):
                next(locals_, None)
                n_local += 1
            while do_scan and n_scan < (i + 1) * grp // n_solve_stages:
                state = scan(tk - 2, n_scan, state)
                n_scan += 1
        for ref, rows, val in stores:
            ref[rows, :] = val
        return state

    state = jnp.zeros((dh, dh), F32)
    state = tick(0, state)
    state = tick(1, state)
    state = lax.fori_loop(2, ngroups, tick, state)
    state = tick(ngroups, state)
    tick(ngroups + 1, state)


def _gated_deltanet(qkvz, gates, conv_w, a_log, dt_bias, o_norm, batch, seq):
    nheads4, n, dh = qkvz.shape
    nh = B_HEADS
    hbk = GDN_HEADS
    nhb = nh // hbk
    c = GDN_CHUNK
    assert nheads4 == 4 * nh and seq % (c * GDN_GROUP) == 0 and nh % hbk == 0
    blk = lambda off: pl.BlockSpec((hbk, seq, dh), lambda b, hb, off=off: (off * nhb + hb, b, 0))
    cw = lambda off: pl.BlockSpec((hbk, CONV_WIDTH, dh), lambda b, hb, off=off: (off * nhb + hb, 0, 0))
    smem = pl.BlockSpec(memory_space=pltpu.SMEM)
    slots = GDN_RING * GDN_GROUP
    scratch = [pltpu.VMEM((slots * c, c), F32), pltpu.VMEM((slots * c, 2 * dh), F32),
               pltpu.VMEM((slots * c, dh), F32), pltpu.VMEM((slots * 2 * c, dh), BF16),
               pltpu.VMEM((slots * c, c), BF16), pltpu.VMEM((slots * c, dh), BF16),
               pltpu.VMEM((slots * SUBLANES, dh), F32)]
    return pl.pallas_call(
        _gdn_kernel,
        grid=(batch, nhb),
        in_specs=[smem, smem, blk(0), blk(1), blk(2), blk(3),
                  pl.BlockSpec((seq, LANES), lambda b, hb: (b, 0)),
                  cw(0), cw(1), cw(2),
                  pl.BlockSpec((1, dh), lambda b, hb: (0, 0))],
        out_specs=pl.BlockSpec((hbk, seq, dh), lambda b, hb: (hb, b, 0)),
        out_shape=jax.ShapeDtypeStruct((nh, n, dh), BF16),
        scratch_shapes=scratch,
        compiler_params=_cparams(("parallel", "parallel")),
        name="gated_deltanet",
    )(a_log, dt_bias, qkvz, qkvz, qkvz, qkvz, gates, conv_w, conv_w, conv_w, o_norm)


def _route_kernel(hn_ref, wr_ref, e1_ref, e2_ref, w1_ref, w2_ref, r1_ref, r2_ref, cnt_ref, base_ref):
    i = pl.program_id(0)
    tm, d = hn_ref.shape[0] // SUBLANES, SUBLANES * LANES
    ne = wr_ref.shape[0]

    @pl.when(i == 0)
    def _():
        base_ref[...] = jnp.zeros_like(base_ref)

    lg = jnp.zeros((ne, tm), F32)
    for s in range(SUBLANES):
        lg = lg + lax.dot_general(wr_ref[:, s * LANES:(s + 1) * LANES], hn_ref[pl.ds(s, tm, stride=SUBLANES), :],
                                  (((1,), (1,)), ((), ())), precision=lax.Precision.HIGHEST,
                                  preferred_element_type=F32)
    eio = lax.broadcasted_iota(I32, (ne, tm), 0)
    m1 = jnp.max(lg, axis=0, keepdims=True)
    i1 = jnp.min(jnp.where(lg == m1, eio, ne), axis=0, keepdims=True)
    lg2 = jnp.where(eio == i1, -jnp.inf, lg)
    m2 = jnp.max(lg2, axis=0, keepdims=True)
    i2 = jnp.min(jnp.where(lg2 == m2, eio, ne), axis=0, keepdims=True)
    ex = jnp.exp(m2 - m1)
    w1 = 1.0 / (1.0 + ex)
    sel = jnp.where((eio == i1) | (eio == i2), 1.0, 0.0)
    tri = jnp.where(lax.broadcasted_iota(I32, (tm, tm), 0) < lax.broadcasted_iota(I32, (tm, tm), 1), 1.0, 0.0)
    pre = jnp.dot(sel.astype(BF16), tri.astype(BF16), preferred_element_type=F32) + base_ref[:, 0:1]
    r1 = jnp.sum(jnp.where(eio == i1, pre, 0.0), axis=0, keepdims=True)
    r2 = jnp.sum(jnp.where(eio == i2, pre, 0.0), axis=0, keepdims=True)
    base_ref[...] += jnp.sum(sel, axis=1, keepdims=True)
    e1_ref[0] = i1
    e2_ref[0] = i2
    w1_ref[0] = w1
    w2_ref[0] = ex * w1
    r1_ref[0] = r1.astype(I32)
    r2_ref[0] = r2.astype(I32)
    cnt_ref[...] = base_ref[...]


def _route(hn_t, wr_t):
    n = hn_t.shape[0] // SUBLANES
    ne = wr_t.shape[0]
    tm = ROW_TILE
    nt = n // tm
    vec = pl.BlockSpec((1, 1, tm), lambda i: (i, 0, 0))
    ish = jax.ShapeDtypeStruct((nt, 1, tm), I32)
    fsh = jax.ShapeDtypeStruct((nt, 1, tm), F32)
    return pl.pallas_call(
        _route_kernel,
        grid=(nt,),
        in_specs=[pl.BlockSpec((tm * SUBLANES, LANES), lambda i: (i, 0)),
                  pl.BlockSpec((ne, SUBLANES * LANES), lambda i: (0, 0))],
        out_specs=[vec, vec, vec, vec, vec, vec, pl.BlockSpec((ne, LANES), lambda i: (0, 0))],
        out_shape=[ish, ish, fsh, fsh, ish, ish, jax.ShapeDtypeStruct((ne, LANES), F32)],
        scratch_shapes=[pltpu.VMEM((ne, LANES), F32)],
        compiler_params=_cparams(("arbitrary",)),
        name="moe_route",
    )(hn_t, wr_t)


def _dispatch_kernel(p1_ref, p2_ref, hn_ref, xs_ref, sem):
    i = pl.program_id(0)
    tg = hn_ref.shape[0] // SUBLANES

    def tile_copy(k, pos):
        src = hn_ref.at[pl.ds(pl.multiple_of(k * SUBLANES, SUBLANES), SUBLANES), :]
        dst = xs_ref.at[pl.ds(pl.multiple_of(pos * SUBLANES, SUBLANES), SUBLANES), :]
        return pltpu.make_async_copy(src, dst, sem)

    def issue(k, carry):
        t = i * tg + k
        tile_copy(k, p1_ref[t]).start()
        tile_copy(k, p2_ref[t]).start()
        return carry

    lax.fori_loop(0, tg, issue, 0, unroll=8)

    def drain(k, carry):
        tile_copy(0, 0).wait()
        tile_copy(0, 0).wait()
        return carry

    lax.fori_loop(0, tg, drain, 0, unroll=8)


def _dispatch(pos1, pos2, hn_t):
    rows8 = hn_t.shape[0]
    n = rows8 // SUBLANES
    tg = ROW_TILE
    grid_spec = pltpu.PrefetchScalarGridSpec(
        num_scalar_prefetch=2,
        grid=(n // tg,),
        in_specs=[pl.BlockSpec((tg * SUBLANES, LANES), lambda i, p1, p2: (i, 0))],
        out_specs=pl.BlockSpec(memory_space=pl.ANY),
        scratch_shapes=[pltpu.SemaphoreType.DMA(())],
    )
    return pl.pallas_call(
        _dispatch_kernel,
        grid_spec=grid_spec,
        out_shape=jax.ShapeDtypeStruct((2 * rows8, LANES), F32),
        compiler_params=_cparams(("arbitrary",)),
        name="moe_dispatch",
    )(pos1, pos2, hn_t)


def _combine_kernel(p1_ref, p2_ref, y_ref, h_ref, w1_ref, w2_ref, g_ref, o_ref, y1_ref, y2_ref, sem):
    i = pl.program_id(0)
    tc, d = h_ref.shape
    slot = i % 2

    def tile_copy(pos, buf, sl, k):
        src = y_ref.at[pl.ds(pl.multiple_of(pos * SUBLANES, SUBLANES), SUBLANES), :]
        dst = buf.at[sl, pl.ds(pl.multiple_of(k * SUBLANES, SUBLANES), SUBLANES), :]
        return pltpu.make_async_copy(src, dst, sem.at[sl])

    def start_gathers(step, sl):
        def issue(k, carry):
            t = step * tc + k
            tile_copy(p1_ref[t], y1_ref, sl, k).start()
            tile_copy(p2_ref[t], y2_ref, sl, k).start()
            return carry

        lax.fori_loop(0, tc, issue, 0, unroll=8)

    @pl.when(i == 0)
    def _():
        start_gathers(0, 0)

    @pl.when(i + 1 < pl.num_programs(0))
    def _():
        start_gathers(i + 1, 1 - slot)

    def drain(k, carry):
        tile_copy(0, y1_ref, slot, 0).wait()
        tile_copy(0, y2_ref, slot, 0).wait()
        return carry

    lax.fori_loop(0, tc, drain, 0, unroll=8)

    w1 = w1_ref[...]
    w2 = w2_ref[...]
    slabs = []
    ss = jnp.zeros((tc, 1), F32)
    for s in range(d // LANES):
        rows = pl.ds(s, tc, stride=SUBLANES)
        hs = h_ref[:, s * LANES:(s + 1) * LANES] + w1 * y1_ref[slot, rows, :] + w2 * y2_ref[slot, rows, :]
        ss = ss + jnp.sum(hs * hs, axis=-1, keepdims=True)
        slabs.append(hs)
    inv = lax.rsqrt(ss / d + EPS)
    for s, hs in enumerate(slabs):
        o_ref[:, s * LANES:(s + 1) * LANES] = hs * inv * g_ref[:, s * LANES:(s + 1) * LANES]


def _combine(pos1, pos2, y_t, h, w1, w2, g):
    n, d = h.shape
    tc = COMBINE_TILE
    row = lambda i, p1, p2: (i, 0)
    grid_spec = pltpu.PrefetchScalarGridSpec(
        num_scalar_prefetch=2,
        grid=(n // tc,),
        in_specs=[pl.BlockSpec(memory_space=pl.ANY),
                  pl.BlockSpec((tc, d), row),
                  pl.BlockSpec((tc, 1), row), pl.BlockSpec((tc, 1), row),
                  pl.BlockSpec((1, d), lambda i, p1, p2: (0, 0))],
        out_specs=pl.BlockSpec((tc, d), row),
        scratch_shapes=[pltpu.VMEM((2, tc * SUBLANES, LANES), F32), pltpu.VMEM((2, tc * SUBLANES, LANES), F32),
                        pltpu.SemaphoreType.DMA((2,))],
    )
    return pl.pallas_call(
        _combine_kernel,
        grid_spec=grid_spec,
        out_shape=jax.ShapeDtypeStruct((n, d), F32),
        compiler_params=_cparams(("arbitrary",)),
        name="moe_combine",
    )(pos1, pos2, y_t, h, w1, w2, g)


def _visit_plan(counts, n_slots):
    ne = counts.shape[0]
    tm = ROW_TILE
    n_tiles = n_slots // tm
    n_visits_max = n_tiles + ne - 1
    ends = jnp.cumsum(counts)
    starts = ends - counts
    t0 = jnp.arange(n_tiles, dtype=I32)[:, None] * tm
    lo = jnp.maximum(starts[None, :], t0)
    hi = jnp.minimum(ends[None, :], t0 + tm)
    hit = (hi > lo).reshape(-1)
    nv = jnp.sum(hit).astype(I32)
    idx = jnp.nonzero(hit, size=n_visits_max, fill_value=0)[0].astype(I32)
    last = idx[jnp.maximum(nv - 1, 0)]
    idx = jnp.where(jnp.arange(n_visits_max) < nv, idx, last)
    vt = idx // ne
    ve = idx % ne
    vlo = starts[ve]
    vhi = ends[ve]
    vfirst = jnp.concatenate([jnp.ones((1,), I32), (vt[1:] != vt[:-1]).astype(I32)])
    return (vt, ve, vlo.astype(I32), vhi.astype(I32), vfirst, nv.reshape(1)), n_visits_max


def _rope_tables(seq):
    half = A_HEAD_DIM // 2
    inv = 1.0 / (ROPE_THETA ** (jnp.arange(0, A_HEAD_DIM, 2, dtype=F32) / A_HEAD_DIM))
    ang = jnp.arange(seq, dtype=F32)[:, None] * inv[None, :]
    cos, sin = jnp.cos(ang), jnp.sin(ang)
    reps = LANES // A_HEAD_DIM
    cos_t = jnp.tile(jnp.concatenate([cos, cos], axis=-1), (1, reps))
    sin_t = jnp.tile(jnp.concatenate([-sin, sin], axis=-1), (1, reps))
    return cos_t, sin_t


def kernel(x, a_norm, a_w_qkv, a_b_qkv, a_sinks, a_w_o, f_norm, f_w_gate, f_w_up, f_w_down,
           b_norm, b_w_in, b_conv_w, b_a_log, b_dt_bias, b_o_norm, b_w_o,
           m_norm, m_w_router, m_w_gate, m_w_up, m_w_down, final_norm):
    batch, seq, d = x.shape
    n = batch * seq
    x2 = x.reshape(n, d)
    row = lambda v: v.reshape(1, -1).astype(F32)

    nq = A_HEADS * A_HEAD_DIM
    nkv = A_KV_HEADS * A_HEAD_DIM
    dup = jnp.repeat(jnp.arange(A_KV_HEADS), 2)[:, None] * A_HEAD_DIM + jnp.arange(A_HEAD_DIM)[None, :]
    cols = jnp.concatenate([jnp.arange(nq), nq + dup.reshape(-1), nq + nkv + dup.reshape(-1)])
    w_qkv = a_w_qkv[0][:, cols].astype(BF16)
    b_qkv = a_b_qkv[0][cols].reshape(1, -1)
    cos_t, sin_t = _rope_tables(seq)
    scale = A_HEAD_DIM ** -0.5
    q, k, v = _qkv_proj(x2, row(a_norm[0]), w_qkv, b_qkv, cos_t * scale, sin_t * scale, cos_t, sin_t, seq)
    att = _attention(q, k, v, a_sinks[0].astype(F32), batch, seq)
    h, hn = _proj_res_norm(att, a_w_o[0].astype(BF16), x2, row(f_norm[0]), tiled_rows=False)
    h, hn = _ffn_dense(hn, f_w_gate[0].astype(BF16), f_w_up[0].astype(BF16), f_w_down[0].astype(BF16),
                       h, row(b_norm[0]))

    nqkvz = 4 * B_HEADS * B_HEAD_DIM
    w_in = b_w_in[0]
    qkvz = _matmul(hn, w_in[:, :nqkvz].astype(BF16), 2 * ROW_TILE, B_HEADS * B_HEAD_DIM, F32, head_major=True)
    w_gates = jnp.pad(w_in[:, nqkvz:], ((0, 0), (0, LANES - 2 * B_HEADS))).astype(BF16)
    gates = _matmul(hn, w_gates, 2 * ROW_TILE, LANES, F32)
    conv_w = b_conv_w[0].astype(F32).reshape(CONV_WIDTH, 3 * B_HEADS, B_HEAD_DIM).transpose(1, 0, 2)
    gdn = _gated_deltanet(qkvz, gates, conv_w, b_a_log[0].astype(F32), b_dt_bias[0].astype(F32),
                          row(b_o_norm[0]), batch, seq)
    h, hn_t = _proj_res_norm(gdn, b_w_o[0].astype(BF16), h, row(m_norm[0]), tiled_rows=True)

    e1, e2, w1, w2, r1, r2, cnt = _route(hn_t, m_w_router[0].T.astype(F32))
    counts = cnt[:, 0].astype(I32)
    starts = jnp.cumsum(counts) - counts
    e1, e2, r1, r2 = (a.reshape(n) for a in (e1, e2, r1, r2))
    pos1 = starts[e1] + r1
    pos2 = starts[e2] + r2
    visits, n_visits_max = _visit_plan(counts, 2 * n)
    xs_t = _dispatch(pos1, pos2, hn_t)
    y_t = _ffn_moe(xs_t, m_w_gate[0].astype(BF16), m_w_up[0].astype(BF16), m_w_down[0].astype(BF16),
                   visits, n_visits_max)
    out = _combine(pos1, pos2, y_t, h, w1.reshape(n, 1), w2.reshape(n, 1), row(final_norm))
    return out.reshape(batch, seq, d)
```

```python
import functools
import math

import jax
import jax.numpy as jnp
from jax import lax
from jax.experimental import pallas as pl
from jax.experimental.pallas import tpu as pltpu

F32 = jnp.float32
BF16 = jnp.bfloat16
I32 = jnp.int32

EPS = 1e-6
LANES = 128
SUBLANES = 8
VMEM_LIMIT = 60 * 1024 * 1024

A_HEADS, A_KV_HEADS, A_HEAD_DIM = 16, 4, 64
WINDOW = 128
ROPE_THETA = 10000.0
B_HEADS, B_HEAD_DIM = 8, 128
CONV_WIDTH = 4
N_EXPERTS = 8
GDN_CHUNK = 128
GDN_GROUP = 8
GDN_RING = 3
GDN_HEADS = 4
NEG = -1e30

ROW_TILE = 512
FF_TILE = 1792
COMBINE_TILE = 256


def _cparams(sem):
    return pltpu.CompilerParams(dimension_semantics=sem, vmem_limit_bytes=VMEM_LIMIT)


def _rms(x, g):
    var = jnp.mean(x * x, axis=-1, keepdims=True)
    return x * lax.rsqrt(var + EPS) * g


def _silu(x):
    return x * (1.0 / (1.0 + jnp.exp(-x)))


def _qkv_kernel(x_ref, g_ref, w_ref, b_ref, cq_ref, sq_ref, ck_ref, sk_ref, q_ref, k_ref, v_ref):
    tm = x_ref.shape[0]
    xn = _rms(x_ref[...], g_ref[...]).astype(BF16)
    lane = lax.broadcasted_iota(I32, (tm, LANES), 1)
    first_half = (lane % A_HEAD_DIM) < (A_HEAD_DIM // 2)

    def rope(xs, c, s):
        sw = jnp.where(first_half, pltpu.roll(xs, LANES - 32, 1), pltpu.roll(xs, 32, 1))
        return xs * c + sw * s

    nq = q_ref.shape[1] // LANES
    nk = k_ref.shape[1] // LANES
    nv = v_ref.shape[1] // LANES
    for s in range(0, nq + nk + nv, 2):
        acc = jnp.dot(xn, w_ref[:, s * LANES:(s + 2) * LANES], preferred_element_type=F32)
        acc = acc + b_ref[:, s * LANES:(s + 2) * LANES]
        for t in range(2):
            col = s + t
            part = acc[:, t * LANES:(t + 1) * LANES]
            if col < nq:
                q_ref[:, col * LANES:(col + 1) * LANES] = rope(part, cq_ref[...], sq_ref[...]).astype(BF16)
            elif col < nq + nk:
                c = col - nq
                k_ref[:, c * LANES:(c + 1) * LANES] = rope(part, ck_ref[...], sk_ref[...]).astype(BF16)
            else:
                c = col - nq - nk
                v_ref[:, c * LANES:(c + 1) * LANES] = part.astype(BF16)


def _qkv_proj(x2, g, w, b, cq, sq, ck, sk, seq):
    n, d = x2.shape
    tm = ROW_TILE
    nq = A_HEADS * A_HEAD_DIM
    nkv = 2 * A_KV_HEADS * A_HEAD_DIM
    tblk = seq // tm
    row = lambda i: (i, 0)
    const = lambda i: (0, 0)
    tab = lambda i: (i % tblk, 0)
    return pl.pallas_call(
        _qkv_kernel,
        grid=(n // tm,),
        in_specs=[
            pl.BlockSpec((tm, d), row),
            pl.BlockSpec((1, d), const),
            pl.BlockSpec((d, nq + 2 * nkv), const),
            pl.BlockSpec((1, nq + 2 * nkv), const),
            pl.BlockSpec((tm, LANES), tab), pl.BlockSpec((tm, LANES), tab),
            pl.BlockSpec((tm, LANES), tab), pl.BlockSpec((tm, LANES), tab),
        ],
        out_specs=[pl.BlockSpec((tm, nq), row), pl.BlockSpec((tm, nkv), row), pl.BlockSpec((tm, nkv), row)],
        out_shape=[jax.ShapeDtypeStruct((n, nq), BF16), jax.ShapeDtypeStruct((n, nkv), BF16),
                   jax.ShapeDtypeStruct((n, nkv), BF16)],
        compiler_params=_cparams(("parallel",)),
        name="qkv_rope",
    )(x2, g, w, b, cq, sq, ck, sk)


def _attn_kernel(sink_ref, q_ref, kp_ref, kc_ref, vp_ref, vc_ref, o_ref):
    n = pl.program_id(1)
    w = WINDOW
    group = A_HEADS // A_KV_HEADS
    rows = group * w
    lo = lax.broadcasted_iota(I32, (w, LANES), 1) < A_HEAD_DIM
    qi = lax.broadcasted_iota(I32, (rows, 2 * w), 0) % w
    kj = lax.broadcasted_iota(I32, (rows, 2 * w), 1)
    valid = (kj > qi) & (kj <= qi + w) & ((n > 0) | (kj >= w))
    hrow = lax.broadcasted_iota(I32, (rows, 1), 0) // w
    zero = jnp.zeros((w, LANES), BF16)
    for g in range(A_KV_HEADS):
        cs = slice(g * LANES, (g + 1) * LANES)
        kd = jnp.concatenate([kp_ref[:, cs], kc_ref[:, cs]], axis=0)
        vd = jnp.concatenate([vp_ref[:, cs], vc_ref[:, cs]], axis=0)
        parts = []
        for p in range(group // 2):
            j = g * (group // 2) + p
            q2 = q_ref[:, j * LANES:(j + 1) * LANES]
            parts.append(jnp.where(lo, q2, zero))
            parts.append(jnp.where(lo, zero, q2))
        qs = jnp.concatenate(parts, axis=0)
        s = lax.dot_general(qs, kd, (((1,), (1,)), ((), ())), preferred_element_type=F32)
        s = jnp.where(valid, s, NEG)
        sk = jnp.full((rows, 1), sink_ref[g * group], F32)
        for h in range(1, group):
            sk = jnp.where(hrow == h, sink_ref[g * group + h], sk)
        m = jnp.maximum(jnp.max(s, axis=-1, keepdims=True), sk)
        p_ = jnp.exp(s - m)
        den = jnp.sum(p_, axis=-1, keepdims=True) + jnp.exp(sk - m)
        o = jnp.dot(p_.astype(BF16), vd, preferred_element_type=F32) / den
        for p in range(group // 2):
            j = g * (group // 2) + p
            o2 = jnp.where(lo, o[2 * p * w:(2 * p + 1) * w], o[(2 * p + 1) * w:(2 * p + 2) * w])
            o_ref[:, j * LANES:(j + 1) * LANES] = o2.astype(BF16)


def _attention(q, k, v, sinks, batch, seq):
    n, dq = q.shape
    dkv = k.shape[1]
    nb = seq // WINDOW
    cur = lambda b, i: (b * nb + i, 0)
    prev = lambda b, i: (b * nb + jnp.maximum(i - 1, 0), 0)
    return pl.pallas_call(
        _attn_kernel,
        grid=(batch, nb),
        in_specs=[
            pl.BlockSpec(memory_space=pltpu.SMEM),
            pl.BlockSpec((WINDOW, dq), cur),
            pl.BlockSpec((WINDOW, dkv), prev), pl.BlockSpec((WINDOW, dkv), cur),
            pl.BlockSpec((WINDOW, dkv), prev), pl.BlockSpec((WINDOW, dkv), cur),
        ],
        out_specs=pl.BlockSpec((WINDOW, dq), cur),
        out_shape=jax.ShapeDtypeStruct((n, dq), BF16),
        compiler_params=_cparams(("parallel", "parallel")),
        name="swa_attention",
    )(sinks, q, k, k, v, v)


def _proj_kernel(a_ref, w_ref, res_ref, g_ref, h_ref, hn_ref, *, tiled_rows):
    if len(a_ref.shape) == 3:
        a = jnp.concatenate([a_ref[j] for j in range(a_ref.shape[0])], axis=-1)
    else:
        a = a_ref[...]
    h = res_ref[...] + jnp.dot(a, w_ref[...], preferred_element_type=F32)
    h_ref[...] = h
    hn = _rms(h, g_ref[...])
    if tiled_rows:
        tm = h.shape[0]
        for s in range(h.shape[1] // LANES):
            hn_ref[pl.ds(s, tm, stride=SUBLANES), :] = hn[:, s * LANES:(s + 1) * LANES]
    else:
        hn_ref[...] = hn.astype(hn_ref.dtype)


def _proj_res_norm(a, w, res, g, *, tiled_rows):
    kdim, d = w.shape
    n = res.shape[0]
    tm = ROW_TILE
    row = lambda i: (i, 0)
    const = lambda i: (0, 0)
    if a.ndim == 3:
        a_spec = pl.BlockSpec((a.shape[0], tm, a.shape[2]), lambda i: (0, i, 0))
    else:
        a_spec = pl.BlockSpec((tm, kdim), row)
    if tiled_rows:
        assert d == SUBLANES * LANES
        hn_spec = pl.BlockSpec((tm * SUBLANES, LANES), row)
        hn_shape = jax.ShapeDtypeStruct((n * SUBLANES, LANES), F32)
    else:
        hn_spec = pl.BlockSpec((tm, d), row)
        hn_shape = jax.ShapeDtypeStruct((n, d), BF16)
    return pl.pallas_call(
        functools.partial(_proj_kernel, tiled_rows=tiled_rows),
        grid=(n // tm,),
        in_specs=[a_spec, pl.BlockSpec((kdim, d), const),
                  pl.BlockSpec((tm, d), row), pl.BlockSpec((1, d), const)],
        out_specs=[pl.BlockSpec((tm, d), row), hn_spec],
        out_shape=[jax.ShapeDtypeStruct((n, d), F32), hn_shape],
        compiler_params=_cparams(("parallel",)),
        name="proj_res_norm",
    )(a, w, res, g)


def _swiglu_partial(xb, wg, wu, wd):
    hg = jnp.dot(xb, wg, preferred_element_type=F32)
    hu = jnp.dot(xb, wu, preferred_element_type=F32)
    act = (_silu(hg) * hu).astype(BF16)
    return jnp.dot(act, wd, preferred_element_type=F32)


def _ffn_dense_kernel(x_ref, wg_ref, wu_ref, wd_ref, res_ref, g_ref, h_ref, hn_ref, acc_ref):
    j = pl.program_id(1)
    last = pl.num_programs(1) - 1
    part = _swiglu_partial(x_ref[...], wg_ref[...], wu_ref[...], wd_ref[...])

    @pl.when(j == 0)
    def _():
        acc_ref[...] = res_ref[...] + part

    @pl.when((j > 0) & (j < last))
    def _():
        acc_ref[...] += part

    @pl.when(j == last)
    def _():
        h = acc_ref[...] + part
        h_ref[...] = h
        hn_ref[...] = _rms(h, g_ref[...]).astype(hn_ref.dtype)


def _ffn_dense(xn, wg, wu, wd, res, g):
    n, d = xn.shape
    ff = wg.shape[1]
    tm, tf = ROW_TILE, FF_TILE
    assert ff % tf == 0 and ff // tf >= 2
    row = lambda i, j: (i, 0)
    return pl.pallas_call(
        _ffn_dense_kernel,
        grid=(n // tm, ff // tf),
        in_specs=[
            pl.BlockSpec((tm, d), row),
            pl.BlockSpec((d, tf), lambda i, j: (0, j)),
            pl.BlockSpec((d, tf), lambda i, j: (0, j)),
            pl.BlockSpec((tf, d), lambda i, j: (j, 0)),
            pl.BlockSpec((tm, d), row),
            pl.BlockSpec((1, d), lambda i, j: (0, 0)),
        ],
        out_specs=[pl.BlockSpec((tm, d), row), pl.BlockSpec((tm, d), row)],
        out_shape=[jax.ShapeDtypeStruct((n, d), F32), jax.ShapeDtypeStruct((n, d), BF16)],
        scratch_shapes=[pltpu.VMEM((tm, d), F32)],
        compiler_params=_cparams(("parallel", "arbitrary")),
        name="swiglu_dense",
    )(xn, wg, wu, wd, res, g)


def _ffn_moe_kernel(vt_ref, ve_ref, vlo_ref, vhi_ref, vfirst_ref, nv_ref,
                    x_ref, wg_ref, wu_ref, wd_ref, y_ref, acc_ref):
    v = pl.program_id(0)
    j = pl.program_id(1)
    last = pl.num_programs(1) - 1
    tm, d = acc_ref.shape
    nslab = d // LANES

    @pl.when(v < nv_ref[0])
    def _():
        slabs = [x_ref[pl.ds(s, tm, stride=SUBLANES), :] for s in range(nslab)]
        xb = jnp.concatenate(slabs, axis=-1).astype(BF16)
        part = _swiglu_partial(xb, wg_ref[...], wu_ref[...], wd_ref[...].astype(BF16))

        @pl.when(j == 0)
        def _():
            acc_ref[...] = part

        @pl.when((j > 0) & (j < last))
        def _():
            acc_ref[...] += part

        @pl.when(j == last)
        def _():
            r = vt_ref[v] * tm + lax.broadcasted_iota(I32, (tm, 1), 0)
            mine = (r >= vlo_ref[v]) & (r < vhi_ref[v])

            @pl.when(vfirst_ref[v] == 1)
            def _():
                for s in range(nslab):
                    cs = slice(s * LANES, (s + 1) * LANES)
                    new = acc_ref[:, cs] + part[:, cs]
                    y_ref[pl.ds(s, tm, stride=SUBLANES), :] = jnp.where(mine, new, 0.0)

            @pl.when(vfirst_ref[v] != 1)
            def _():
                for s in range(nslab):
                    cs = slice(s * LANES, (s + 1) * LANES)
                    rows = pl.ds(s, tm, stride=SUBLANES)
                    new = acc_ref[:, cs] + part[:, cs]
                    y_ref[rows, :] = jnp.where(mine, new, y_ref[rows, :])


def _ffn_moe(xs_t, wg, wu, wd, visits, n_visits_max):
    vt, ve, vlo, vhi, vfirst, nv = visits
    rows8, _ = xs_t.shape
    p = rows8 // SUBLANES
    d = wg.shape[1]
    ff = wg.shape[2]
    tm, tf = ROW_TILE, FF_TILE
    nf = ff // tf
    assert ff % tf == 0 and nf >= 2

    def jj(v, j, nv_ref):
        return jnp.where(v < nv_ref[0], j, nf - 1)

    xmap = lambda v, j, vt, ve, vlo, vhi, vf, nv: (vt[v], 0)
    gmap = lambda v, j, vt, ve, vlo, vhi, vf, nv: (ve[v], 0, jj(v, j, nv))
    dmap = lambda v, j, vt, ve, vlo, vhi, vf, nv: (ve[v], jj(v, j, nv), 0)
    grid_spec = pltpu.PrefetchScalarGridSpec(
        num_scalar_prefetch=6,
        grid=(n_visits_max, nf),
        in_specs=[
            pl.BlockSpec((tm * SUBLANES, LANES), xmap),
            pl.BlockSpec((None, d, tf), gmap),
            pl.BlockSpec((None, d, tf), gmap),
            pl.BlockSpec((None, tf, d), dmap),
        ],
        out_specs=pl.BlockSpec((tm * SUBLANES, LANES), xmap),
        scratch_shapes=[pltpu.VMEM((tm, d), F32)],
    )
    return pl.pallas_call(
        _ffn_moe_kernel,
        grid_spec=grid_spec,
        out_shape=jax.ShapeDtypeStruct((p * SUBLANES, LANES), F32),
        compiler_params=_cparams(("arbitrary", "arbitrary")),
        name="swiglu_moe",
    )(vt, ve, vlo, vhi, vfirst, nv, xs_t, wg, wu, wd)


def _mm_kernel(x_ref, w_ref, o_ref):
    acc = jnp.dot(x_ref[...], w_ref[...], preferred_element_type=F32)
    if len(o_ref.shape) == 3:
        for j in range(o_ref.shape[0]):
            o_ref[j] = acc[:, j * LANES:(j + 1) * LANES].astype(o_ref.dtype)
    else:
        o_ref[...] = acc.astype(o_ref.dtype)


def _matmul(x, w, tm, tn, out_dtype, *, head_major=False):
    n, kdim = x.shape
    m = w.shape[1]
    if head_major:
        out_spec = pl.BlockSpec((tn // LANES, tm, LANES), lambda c, r: (c, r, 0))
        out_shape = jax.ShapeDtypeStruct((m // LANES, n, LANES), out_dtype)
    else:
        out_spec = pl.BlockSpec((tm, tn), lambda c, r: (r, c))
        out_shape = jax.ShapeDtypeStruct((n, m), out_dtype)
    return pl.pallas_call(
        _mm_kernel,
        grid=(m // tn, n // tm),
        in_specs=[pl.BlockSpec((tm, kdim), lambda c, r: (r, 0)), pl.BlockSpec((kdim, tn), lambda c, r: (0, c))],
        out_specs=out_spec,
        out_shape=out_shape,
        compiler_params=_cparams(("parallel", "parallel")),
        name="matmul",
    )(x, w)


def _gdn_kernel(alog_ref, dtb_ref, q_ref, k_ref, v_ref, z_ref, gt_ref, cwq_ref, cwk_ref, cwv_ref, on_ref,
                o_ref, lo_ref, rhs_ref, u_ref, wq_ref, in_ref, kd_ref, gl_ref):
    hb = pl.program_id(1)
    nh, t, dh = q_ref.shape
    c = GDN_CHUNK
    grp = GDN_GROUP
    gph = t // (c * grp)
    ngroups = nh * gph
    halo = SUBLANES
    ii = lax.broadcasted_iota(I32, (c, c), 0)
    jx = lax.broadcasted_iota(I32, (c, c), 1)
    causal = ii >= jx
    strict = ii > jx
    bits = ii ^ jx
    pos = lax.broadcasted_iota(I32, (c, dh), 0)
    nt = (((1,), (1,)), ((), ()))
    tn = (((0,), (0,)), ((), ()))

    def slot_rows(tk, k, rows_per_chunk=c):
        base = ((tk % GDN_RING) * grp + k) * rows_per_chunk
        return pl.ds(pl.multiple_of(base, rows_per_chunk), rows_per_chunk)

    def conv_silu(x_ref, w_ref, hh, ci, rs, rp):
        x = x_ref[hh, rs, :]
        prev = jnp.where(ci > 0, x_ref[hh, rp, :], 0.0)
        xx = jnp.concatenate([prev, x], axis=0)
        w = w_ref[hh]
        y = x * w[CONV_WIDTH - 1:CONV_WIDTH, :]
        for s in range(1, CONV_WIDTH):
            y = y + pltpu.roll(xx, s, 0)[halo:] * w[CONV_WIDTH - 1 - s:CONV_WIDTH - s, :]
        return _silu(y)

    def l2n(x):
        return x * lax.rsqrt(jnp.sum(x * x, axis=-1, keepdims=True) + EPS)

    def local(tk, stores):
        hh = tk // gph
        head = hb * nh + hh
        neg_a = -jnp.exp(alog_ref[head])
        dtb = dtb_ref[head]
        for k in range(grp):
            ci = (tk % gph) * grp + k
            r0 = pl.multiple_of(ci * c, c)
            rs = pl.ds(r0, c)
            rp = pl.ds(pl.multiple_of(jnp.maximum(r0 - halo, 0), halo), halo)
            qc = l2n(conv_silu(q_ref, cwq_ref, hh, ci, rs, rp)) * (dh ** -0.5)
            kc = l2n(conv_silu(k_ref, cwk_ref, hh, ci, rs, rp))
            vc = conv_silu(v_ref, cwv_ref, hh, ci, rs, rp)
            gt = gt_ref[rs, :]
            lane = lax.broadcasted_iota(I32, gt.shape, 1)
            b_log = jnp.sum(jnp.where(lane == head, gt, 0.0), axis=-1, keepdims=True)
            a_log = jnp.sum(jnp.where(lane == head + B_HEADS, gt, 0.0), axis=-1, keepdims=True)
            beta = jnp.broadcast_to(1.0 / (1.0 + jnp.exp(-b_log)), (c, dh))
            xa = a_log + dtb
            softplus = jnp.maximum(xa, 0.0) + jnp.log(1.0 + jnp.exp(-jnp.abs(xa)))
            gcb = jnp.broadcast_to(neg_a * softplus, (c, dh))
            sh = 1
            while sh < c:
                gcb = gcb + jnp.where(pos >= sh, pltpu.roll(gcb, sh, 0), 0.0)
                sh *= 2
            diff = gcb - gcb.T
            decay = jnp.exp(jnp.where(causal, diff, NEG))
            kb = kc * beta
            both = lax.dot_general(jnp.concatenate([kb.astype(BF16), qc.astype(BF16)], axis=0),
                                   kc.astype(BF16), nt, preferred_element_type=F32)
            eg = jnp.exp(gcb)
            glast = gcb[c - 1:c, :]
            stores.append((lo_ref, slot_rows(tk, k), jnp.where(strict, both[:c] * decay, 0.0).astype(BF16)))
            stores.append((rhs_ref, slot_rows(tk, k), jnp.concatenate([vc * beta, kb * eg], axis=-1)))
            stores.append((wq_ref, pl.ds(pl.multiple_of((((tk % GDN_RING) * grp + k) * 2 + 1) * c, c), c),
                           (qc * eg).astype(BF16)))
            stores.append((in_ref, slot_rows(tk, k), jnp.where(causal, both[c:] * decay, 0.0).astype(BF16)))
            stores.append((kd_ref, slot_rows(tk, k), (kc * jnp.exp(glast - gcb)).astype(BF16)))
            stores.append((gl_ref, slot_rows(tk, k, halo), jnp.broadcast_to(jnp.exp(glast), (halo, dh))))
            yield

    def solve(tk, stores):
        def mask(cond):
            return jnp.where(cond, 1.0, 0.0).astype(BF16)

        eye_b = mask(ii == jx)
        lowers = [lo_ref[slot_rows(tk, k), :] for k in range(grp)]
        base = mask(bits < 2)
        tbs = [eye_b - lower * base for lower in lowers]
        yield
        m = 2
        while m < c:
            level = mask((bits >= m) & (bits < 2 * m))
            ys = [jnp.dot(lower * level, tb, preferred_element_type=F32).astype(BF16)
                  for lower, tb in zip(lowers, tbs)]
            yield
            tbs = [tb - jnp.dot(tb, y, preferred_element_type=F32).astype(BF16) for tb, y in zip(tbs, ys)]
            yield
            m *= 2
        for k, tb in enumerate(tbs):
            rhs = rhs_ref[slot_rows(tk, k), :]
            x = rhs + jnp.dot(tb - eye_b, rhs.astype(BF16), preferred_element_type=F32)
            stores.append((u_ref, slot_rows(tk, k), x[:, :dh]))
            stores.append((wq_ref, pl.ds(pl.multiple_of(((tk % GDN_RING) * grp + k) * 2 * c, c), c),
                           x[:, dh:].astype(BF16)))
        yield

    def scan(tk, k, state):
        hh = tk // gph
        ci = (tk % gph) * grp + k
        rs = pl.ds(pl.multiple_of(ci * c, c), c)
        if k == 0:
            state = jnp.where(tk % gph == 0, 0.0, state)
        wq = wq_ref[pl.ds(pl.multiple_of(((tk % GDN_RING) * grp + k) * 2 * c, 2 * c), 2 * c), :]
        xs = jnp.dot(wq, state.astype(BF16), preferred_element_type=F32)
        vnew = (u_ref[slot_rows(tk, k), :] - xs[:c]).astype(BF16)
        o = xs[c:] + jnp.dot(in_ref[slot_rows(tk, k), :], vnew, preferred_element_type=F32)
        dec = gl_ref[pl.ds(pl.multiple_of(((tk % GDN_RING) * grp + k) * halo, halo), 1), :]
        state = state * dec + lax.dot_general(kd_ref[slot_rows(tk, k), :], vnew, tn, preferred_element_type=F32)
        on = _rms(o, on_ref[...]) * _silu(z_ref[hh, rs, :])
        o_ref[hh, rs, :] = on.astype(o_ref.dtype)
        return state

    n_solve_stages = 2 * (c.bit_length() - 2) + 2

    def tick(tk, state):
        static = isinstance(tk, int)
        do_local = (not static) or tk < ngroups
        do_solve = (not static) or 1 <= tk <= ngroups
        do_scan = (not static) or 2 <= tk <= ngroups + 1
        tk = jnp.int32(tk)
        stores = []
        solver = solve(tk - 1, stores) if do_solve else iter(())
        locals_ = local(tk, stores) if do_local else iter(())
        n_local = n_scan = 0
        for i in range(n_solve_stages):
            next(solver, None)
            while n_local < -((i + 1) * grp // -n_solve_stages):
                next(locals_, None)
                n_local += 1
            while do_scan and n_scan < (i + 1) * grp // n_solve_stages:
                state = scan(tk - 2, n_scan, state)
                n_scan += 1
        for ref, rows, val in stores:
            ref[rows, :] = val
        return state

    state = jnp.zeros((dh, dh), F32)
    state = tick(0, state)
    state = tick(1, state)
    state = lax.fori_loop(2, ngroups, tick, state)
    state = tick(ngroups, state)
    tick(ngroups + 1, state)


def _gated_deltanet(qkvz, gates, conv_w, a_log, dt_bias, o_norm, batch, seq):
    nheads4, n, dh = qkvz.shape
    nh = B_HEADS
    hbk = GDN_HEADS
    nhb = nh // hbk
    c = GDN_CHUNK
    assert nheads4 == 4 * nh and seq % (c * GDN_GROUP) == 0 and nh % hbk == 0
    blk = lambda off: pl.BlockSpec((hbk, seq, dh), lambda b, hb, off=off: (off * nhb + hb, b, 0))
    cw = lambda off: pl.BlockSpec((hbk, CONV_WIDTH, dh), lambda b, hb, off=off: (off * nhb + hb, 0, 0))
    smem = pl.BlockSpec(memory_space=pltpu.SMEM)
    slots = GDN_RING * GDN_GROUP
    scratch = [pltpu.VMEM((slots * c, c), BF16), pltpu.VMEM((slots * c, 2 * dh), F32),
               pltpu.VMEM((slots * c, dh), F32), pltpu.VMEM((slots * 2 * c, dh), BF16),
               pltpu.VMEM((slots * c, c), BF16), pltpu.VMEM((slots * c, dh), BF16),
               pltpu.VMEM((slots * SUBLANES, dh), F32)]
    return pl.pallas_call(
        _gdn_kernel,
        grid=(batch, nhb),
        in_specs=[smem, smem, blk(0), blk(1), blk(2), blk(3),
                  pl.BlockSpec((seq, LANES), lambda b, hb: (b, 0)),
                  cw(0), cw(1), cw(2),
                  pl.BlockSpec((1, dh), lambda b, hb: (0, 0))],
        out_specs=pl.BlockSpec((hbk, seq, dh), lambda b, hb: (hb, b, 0)),
        out_shape=jax.ShapeDtypeStruct((nh, n, dh), BF16),
        scratch_shapes=scratch,
        compiler_params=_cparams(("parallel", "parallel")),
        name="gated_deltanet",
    )(a_log, dt_bias, qkvz, qkvz, qkvz, qkvz, gates, conv_w, conv_w, conv_w, o_norm)


def _route_kernel(hn_ref, wr_ref, e1_ref, e2_ref, w1_ref, w2_ref, r1_ref, r2_ref, cnt_ref, base_ref):
    i = pl.program_id(0)
    tm, d = hn_ref.shape[0] // SUBLANES, SUBLANES * LANES
    ne = wr_ref.shape[0]

    @pl.when(i == 0)
    def _():
        base_ref[...] = jnp.zeros_like(base_ref)

    lg = jnp.zeros((ne, tm), F32)
    for s in range(SUBLANES):
        lg = lg + lax.dot_general(wr_ref[:, s * LANES:(s + 1) * LANES], hn_ref[pl.ds(s, tm, stride=SUBLANES), :],
                                  (((1,), (1,)), ((), ())), precision=lax.Precision.HIGHEST,
                                  preferred_element_type=F32)
    eio = lax.broadcasted_iota(I32, (ne, tm), 0)
    m1 = jnp.max(lg, axis=0, keepdims=True)
    i1 = jnp.min(jnp.where(lg == m1, eio, ne), axis=0, keepdims=True)
    lg2 = jnp.where(eio == i1, -jnp.inf, lg)
    m2 = jnp.max(lg2, axis=0, keepdims=True)
    i2 = jnp.min(jnp.where(lg2 == m2, eio, ne), axis=0, keepdims=True)
    ex = jnp.exp(m2 - m1)
    w1 = 1.0 / (1.0 + ex)
    sel = jnp.where((eio == i1) | (eio == i2), 1.0, 0.0)
    tri = jnp.where(lax.broadcasted_iota(I32, (tm, tm), 0) < lax.broadcasted_iota(I32, (tm, tm), 1), 1.0, 0.0)
    pre = jnp.dot(sel.astype(BF16), tri.astype(BF16), preferred_element_type=F32) + base_ref[:, 0:1]
    r1 = jnp.sum(jnp.where(eio == i1, pre, 0.0), axis=0, keepdims=True)
    r2 = jnp.sum(jnp.where(eio == i2, pre, 0.0), axis=0, keepdims=True)
    base_ref[...] += jnp.sum(sel, axis=1, keepdims=True)
    e1_ref[0] = i1
    e2_ref[0] = i2
    w1_ref[0] = w1
    w2_ref[0] = ex * w1
    r1_ref[0] = r1.astype(I32)
    r2_ref[0] = r2.astype(I32)
    cnt_ref[...] = base_ref[...]


def _route(hn_t, wr_t):
    n = hn_t.shape[0] // SUBLANES
    ne = wr_t.shape[0]
    tm = ROW_TILE
    nt = n // tm
    vec = pl.BlockSpec((1, 1, tm), lambda i: (i, 0, 0))
    ish = jax.ShapeDtypeStruct((nt, 1, tm), I32)
    fsh = jax.ShapeDtypeStruct((nt, 1, tm), F32)
    return pl.pallas_call(
        _route_kernel,
        grid=(nt,),
        in_specs=[pl.BlockSpec((tm * SUBLANES, LANES), lambda i: (i, 0)),
                  pl.BlockSpec((ne, SUBLANES * LANES), lambda i: (0, 0))],
        out_specs=[vec, vec, vec, vec, vec, vec, pl.BlockSpec((ne, LANES), lambda i: (0, 0))],
        out_shape=[ish, ish, fsh, fsh, ish, ish, jax.ShapeDtypeStruct((ne, LANES), F32)],
        scratch_shapes=[pltpu.VMEM((ne, LANES), F32)],
        compiler_params=_cparams(("arbitrary",)),
        name="moe_route",
    )(hn_t, wr_t)


def _dispatch_kernel(p1_ref, p2_ref, hn_ref, xs_ref, sem):
    i = pl.program_id(0)
    tg = hn_ref.shape[0] // SUBLANES

    def tile_copy(k, pos):
        src = hn_ref.at[pl.ds(pl.multiple_of(k * SUBLANES, SUBLANES), SUBLANES), :]
        dst = xs_ref.at[pl.ds(pl.multiple_of(pos * SUBLANES, SUBLANES), SUBLANES), :]
        return pltpu.make_async_copy(src, dst, sem)

    def issue(k, carry):
        t = i * tg + k
        tile_copy(k, p1_ref[t]).start(priority=0)
        tile_copy(k, p2_ref[t]).start(priority=1)
        return carry

    lax.fori_loop(0, tg, issue, 0, unroll=8)

    whole = pltpu.make_async_copy(hn_ref, xs_ref.at[pl.ds(0, tg * SUBLANES), :], sem)
    whole.wait()
    whole.wait()


def _dispatch(pos1, pos2, hn_t):
    rows8 = hn_t.shape[0]
    n = rows8 // SUBLANES
    tg = ROW_TILE
    grid_spec = pltpu.PrefetchScalarGridSpec(
        num_scalar_prefetch=2,
        grid=(n // tg,),
        in_specs=[pl.BlockSpec((tg * SUBLANES, LANES), lambda i, p1, p2: (i, 0))],
        out_specs=pl.BlockSpec(memory_space=pl.ANY),
        scratch_shapes=[pltpu.SemaphoreType.DMA(())],
    )
    return pl.pallas_call(
        _dispatch_kernel,
        grid_spec=grid_spec,
        out_shape=jax.ShapeDtypeStruct((2 * rows8, LANES), F32),
        compiler_params=_cparams(("arbitrary",)),
        name="moe_dispatch",
    )(pos1, pos2, hn_t)


def _combine_kernel(p1_ref, p2_ref, y_ref, h_ref, w1_ref, w2_ref, g_ref, o_ref, y1_ref, y2_ref, sem):
    i = pl.program_id(0)
    nsteps = pl.num_programs(0)
    tc, d = h_ref.shape
    nslab = d // LANES
    slot = i % 2

    def tile_copy(pos, buf, sl, k):
        src = y_ref.at[pl.ds(pl.multiple_of(pos * SUBLANES, SUBLANES), SUBLANES), :]
        dst = buf.at[sl, pl.ds(pl.multiple_of(k * SUBLANES, SUBLANES), SUBLANES), :]
        return pltpu.make_async_copy(src, dst, sem.at[sl])

    def start_gather(step, sl, k):
        t = step * tc + k
        tile_copy(p1_ref[t], y1_ref, sl, k).start(priority=0)
        tile_copy(p2_ref[t], y2_ref, sl, k).start(priority=1)

    def wait_slot(sl):
        whole = pl.ds(0, tc * SUBLANES)
        pltpu.make_async_copy(y_ref.at[whole, :], y1_ref.at[sl], sem.at[sl]).wait()
        pltpu.make_async_copy(y_ref.at[whole, :], y2_ref.at[sl], sem.at[sl]).wait()

    @pl.when(i == 0)
    def _():
        def issue(k, carry):
            start_gather(0, 0, k)
            return carry

        lax.fori_loop(0, tc, issue, 0, unroll=8)

    wait_slot(slot)

    nxt = jnp.minimum(i + 1, nsteps - 1)
    per_slab = tc // nslab
    w1 = w1_ref[...]
    w2 = w2_ref[...]
    slabs = []
    ss = jnp.zeros((tc, 1), F32)
    for s in range(nslab):
        for k in range(s * per_slab, (s + 1) * per_slab):
            start_gather(nxt, 1 - slot, k)
        rows = pl.ds(s, tc, stride=SUBLANES)
        hs = h_ref[:, s * LANES:(s + 1) * LANES] + w1 * y1_ref[slot, rows, :] + w2 * y2_ref[slot, rows, :]
        ss = ss + jnp.sum(hs * hs, axis=-1, keepdims=True)
        slabs.append(hs)
    inv = lax.rsqrt(ss / d + EPS)
    for s, hs in enumerate(slabs):
        o_ref[:, s * LANES:(s + 1) * LANES] = hs * inv * g_ref[:, s * LANES:(s + 1) * LANES]

    @pl.when(i == nsteps - 1)
    def _():
        wait_slot(1 - slot)


def _combine(pos1, pos2, y_t, h, w1, w2, g):
    n, d = h.shape
    tc = COMBINE_TILE
    row = lambda i, p1, p2: (i, 0)
    grid_spec = pltpu.PrefetchScalarGridSpec(
        num_scalar_prefetch=2,
        grid=(n // tc,),
        in_specs=[pl.BlockSpec(memory_space=pl.ANY),
                  pl.BlockSpec((tc, d), row),
                  pl.BlockSpec((tc, 1), row), pl.BlockSpec((tc, 1), row),
                  pl.BlockSpec((1, d), lambda i, p1, p2: (0, 0))],
        out_specs=pl.BlockSpec((tc, d), row),
        scratch_shapes=[pltpu.VMEM((2, tc * SUBLANES, LANES), F32), pltpu.VMEM((2, tc * SUBLANES, LANES), F32),
                        pltpu.SemaphoreType.DMA((2,))],
    )
    return pl.pallas_call(
        _combine_kernel,
        grid_spec=grid_spec,
        out_shape=jax.ShapeDtypeStruct((n, d), F32),
        compiler_params=_cparams(("arbitrary",)),
        name="moe_combine",
    )(pos1, pos2, y_t, h, w1, w2, g)


def _visit_plan(counts, n_slots):
    ne = counts.shape[0]
    tm = ROW_TILE
    n_tiles = n_slots // tm
    n_visits_max = n_tiles + ne - 1
    ends = jnp.cumsum(counts)
    starts = ends - counts
    t0 = jnp.arange(n_tiles, dtype=I32)[:, None] * tm
    lo = jnp.maximum(starts[None, :], t0)
    hi = jnp.minimum(ends[None, :], t0 + tm)
    hit = (hi > lo).reshape(-1)
    nv = jnp.sum(hit).astype(I32)
    idx = jnp.nonzero(hit, size=n_visits_max, fill_value=0)[0].astype(I32)
    last = idx[jnp.maximum(nv - 1, 0)]
    idx = jnp.where(jnp.arange(n_visits_max) < nv, idx, last)
    vt = idx // ne
    ve = idx % ne
    vlo = starts[ve]
    vhi = ends[ve]
    vfirst = jnp.concatenate([jnp.ones((1,), I32), (vt[1:] != vt[:-1]).astype(I32)])
    return (vt, ve, vlo.astype(I32), vhi.astype(I32), vfirst, nv.reshape(1)), n_visits_max


def _rope_tables(seq):
    half = A_HEAD_DIM // 2
    inv = 1.0 / (ROPE_THETA ** (jnp.arange(0, A_HEAD_DIM, 2, dtype=F32) / A_HEAD_DIM))
    ang = jnp.arange(seq, dtype=F32)[:, None] * inv[None, :]
    cos, sin = jnp.cos(ang), jnp.sin(ang)
    reps = LANES // A_HEAD_DIM
    cos_t = jnp.tile(jnp.concatenate([cos, cos], axis=-1), (1, reps))
    sin_t = jnp.tile(jnp.concatenate([-sin, sin], axis=-1), (1, reps))
    return cos_t, sin_t


def kernel(x, a_norm, a_w_qkv, a_b_qkv, a_sinks, a_w_o, f_norm, f_w_gate, f_w_up, f_w_down,
           b_norm, b_w_in, b_conv_w, b_a_log, b_dt_bias, b_o_norm, b_w_o,
           m_norm, m_w_router, m_w_gate, m_w_up, m_w_down, final_norm):
    batch, seq, d = x.shape
    n = batch * seq
    x2 = x.reshape(n, d)
    row = lambda v: v.reshape(1, -1).astype(F32)

    nq = A_HEADS * A_HEAD_DIM
    nkv = A_KV_HEADS * A_HEAD_DIM
    dup = jnp.repeat(jnp.arange(A_KV_HEADS), 2)[:, None] * A_HEAD_DIM + jnp.arange(A_HEAD_DIM)[None, :]
    cols = jnp.concatenate([jnp.arange(nq), nq + dup.reshape(-1), nq + nkv + dup.reshape(-1)])
    w_qkv = a_w_qkv[0][:, cols].astype(BF16)
    b_qkv = a_b_qkv[0][cols].reshape(1, -1)
    cos_t, sin_t = _rope_tables(seq)
    scale = A_HEAD_DIM ** -0.5
    q, k, v = _qkv_proj(x2, row(a_norm[0]), w_qkv, b_qkv, cos_t * scale, sin_t * scale, cos_t, sin_t, seq)
    att = _attention(q, k, v, a_sinks[0].astype(F32), batch, seq)
    h, hn = _proj_res_norm(att, a_w_o[0].astype(BF16), x2, row(f_norm[0]), tiled_rows=False)
    h, hn = _ffn_dense(hn, f_w_gate[0].astype(BF16), f_w_up[0].astype(BF16), f_w_down[0].astype(BF16),
                       h, row(b_norm[0]))

    nqkvz = 4 * B_HEADS * B_HEAD_DIM
    w_in = b_w_in[0]
    qkvz = _matmul(hn, w_in[:, :nqkvz].astype(BF16), 2 * ROW_TILE, B_HEADS * B_HEAD_DIM, F32, head_major=True)
    w_gates = jnp.pad(w_in[:, nqkvz:], ((0, 0), (0, LANES - 2 * B_HEADS))).astype(BF16)
    gates = _matmul(hn, w_gates, 2 * ROW_TILE, LANES, F32)
    conv_w = b_conv_w[0].astype(F32).reshape(CONV_WIDTH, 3 * B_HEADS, B_HEAD_DIM).transpose(1, 0, 2)
    gdn = _gated_deltanet(qkvz, gates, conv_w, b_a_log[0].astype(F32), b_dt_bias[0].astype(F32),
                          row(b_o_norm[0]), batch, seq)
    h, hn_t = _proj_res_norm(gdn, b_w_o[0].astype(BF16), h, row(m_norm[0]), tiled_rows=True)

    e1, e2, w1, w2, r1, r2, cnt = _route(hn_t, m_w_router[0].T.astype(F32))
    counts = cnt[:, 0].astype(I32)
    starts = jnp.cumsum(counts) - counts
    e1, e2, r1, r2 = (a.reshape(n) for a in (e1, e2, r1, r2))
    pos1 = starts[e1] + r1
    pos2 = starts[e2] + r2
    visits, n_visits_max = _visit_plan(counts, 2 * n)
    xs_t = _dispatch(pos1, pos2, hn_t)
    y_t = _ffn_moe(xs_t, m_w_gate[0].astype(BF16), m_w_up[0].astype(BF16), m_w_down[0], visits, n_visits_max)
    out = _combine(pos1, pos2, y_t, h, w1.reshape(n, 1), w2.reshape(n, 1), row(final_norm))
    return out.reshape(batch, seq, d)
```

```python
import functools
import math

import jax
import jax.numpy as jnp
from jax import lax
from jax.experimental import pallas as pl
from jax.experimental.pallas import tpu as pltpu

F32 = jnp.float32
BF16 = jnp.bfloat16
I32 = jnp.int32

EPS = 1e-6
LANES = 128
SUBLANES = 8
VMEM_LIMIT = 60 * 1024 * 1024

A_HEADS, A_KV_HEADS, A_HEAD_DIM = 16, 4, 64
WINDOW = 128
ROPE_THETA = 10000.0
B_HEADS, B_HEAD_DIM = 8, 128
CONV_WIDTH = 4
N_EXPERTS = 8
GDN_CHUNK = 128
GDN_GROUP = 8
GDN_RING = 3
GDN_HEADS = 4
NEG = -1e30

ROW_TILE = 512
FF_TILE = 1792
COMBINE_TILE = 256


def _cparams(sem):
    return pltpu.CompilerParams(dimension_semantics=sem, vmem_limit_bytes=VMEM_LIMIT)


def _rms(x, g):
    var = jnp.mean(x * x, axis=-1, keepdims=True)
    return x * lax.rsqrt(var + EPS) * g


def _silu(x):
    return x * (1.0 / (1.0 + jnp.exp(-x)))


def _qkv_kernel(x_ref, g_ref, w_ref, b_ref, cq_ref, sq_ref, ck_ref, sk_ref, q_ref, k_ref, v_ref):
    tm = x_ref.shape[0]
    xn = _rms(x_ref[...], g_ref[...]).astype(BF16)
    lane = lax.broadcasted_iota(I32, (tm, LANES), 1)
    first_half = (lane % A_HEAD_DIM) < (A_HEAD_DIM // 2)

    def rope(xs, c, s):
        sw = jnp.where(first_half, pltpu.roll(xs, LANES - 32, 1), pltpu.roll(xs, 32, 1))
        return xs * c + sw * s

    nq = q_ref.shape[1] // LANES
    nk = k_ref.shape[1] // LANES
    nv = v_ref.shape[1] // LANES
    for s in range(0, nq + nk + nv, 2):
        acc = jnp.dot(xn, w_ref[:, s * LANES:(s + 2) * LANES], preferred_element_type=F32)
        acc = acc + b_ref[:, s * LANES:(s + 2) * LANES]
        for t in range(2):
            col = s + t
            part = acc[:, t * LANES:(t + 1) * LANES]
            if col < nq:
                q_ref[:, col * LANES:(col + 1) * LANES] = rope(part, cq_ref[...], sq_ref[...]).astype(BF16)
            elif col < nq + nk:
                c = col - nq
                k_ref[:, c * LANES:(c + 1) * LANES] = rope(part, ck_ref[...], sk_ref[...]).astype(BF16)
            else:
                c = col - nq - nk
                v_ref[:, c * LANES:(c + 1) * LANES] = part.astype(BF16)


def _qkv_proj(x2, g, w, b, cq, sq, ck, sk, seq):
    n, d = x2.shape
    tm = ROW_TILE
    nq = A_HEADS * A_HEAD_DIM
    nkv = 2 * A_KV_HEADS * A_HEAD_DIM
    tblk = seq // tm
    row = lambda i: (i, 0)
    const = lambda i: (0, 0)
    tab = lambda i: (i % tblk, 0)
    return pl.pallas_call(
        _qkv_kernel,
        grid=(n // tm,),
        in_specs=[
            pl.BlockSpec((tm, d), row),
            pl.BlockSpec((1, d), const),
            pl.BlockSpec((d, nq + 2 * nkv), const),
            pl.BlockSpec((1, nq + 2 * nkv), const),
            pl.BlockSpec((tm, LANES), tab), pl.BlockSpec((tm, LANES), tab),
            pl.BlockSpec((tm, LANES), tab), pl.BlockSpec((tm, LANES), tab),
        ],
        out_specs=[pl.BlockSpec((tm, nq), row), pl.BlockSpec((tm, nkv), row), pl.BlockSpec((tm, nkv), row)],
        out_shape=[jax.ShapeDtypeStruct((n, nq), BF16), jax.ShapeDtypeStruct((n, nkv), BF16),
                   jax.ShapeDtypeStruct((n, nkv), BF16)],
        compiler_params=_cparams(("parallel",)),
        name="qkv_rope",
    )(x2, g, w, b, cq, sq, ck, sk)


def _attn_kernel(sink_ref, q_ref, kp_ref, kc_ref, vp_ref, vc_ref, o_ref):
    n = pl.program_id(1)
    w = WINDOW
    group = A_HEADS // A_KV_HEADS
    rows = group * w
    lo = lax.broadcasted_iota(I32, (w, LANES), 1) < A_HEAD_DIM
    qi = lax.broadcasted_iota(I32, (rows, 2 * w), 0) % w
    kj = lax.broadcasted_iota(I32, (rows, 2 * w), 1)
    valid = (kj > qi) & (kj <= qi + w) & ((n > 0) | (kj >= w))
    hrow = lax.broadcasted_iota(I32, (rows, 1), 0) // w
    zero = jnp.zeros((w, LANES), BF16)
    for g in range(A_KV_HEADS):
        cs = slice(g * LANES, (g + 1) * LANES)
        kd = jnp.concatenate([kp_ref[:, cs], kc_ref[:, cs]], axis=0)
        vd = jnp.concatenate([vp_ref[:, cs], vc_ref[:, cs]], axis=0)
        parts = []
        for p in range(group // 2):
            j = g * (group // 2) + p
            q2 = q_ref[:, j * LANES:(j + 1) * LANES]
            parts.append(jnp.where(lo, q2, zero))
            parts.append(jnp.where(lo, zero, q2))
        qs = jnp.concatenate(parts, axis=0)
        s = lax.dot_general(qs, kd, (((1,), (1,)), ((), ())), preferred_element_type=F32)
        s = jnp.where(valid, s, NEG)
        sk = jnp.full((rows, 1), sink_ref[g * group], F32)
        for h in range(1, group):
            sk = jnp.where(hrow == h, sink_ref[g * group + h], sk)
        m = jnp.maximum(jnp.max(s, axis=-1, keepdims=True), sk)
        p_ = jnp.exp(s - m)
        den = jnp.sum(p_, axis=-1, keepdims=True) + jnp.exp(sk - m)
        o = jnp.dot(p_.astype(BF16), vd, preferred_element_type=F32) / den
        for p in range(group // 2):
            j = g * (group // 2) + p
            o2 = jnp.where(lo, o[2 * p * w:(2 * p + 1) * w], o[(2 * p + 1) * w:(2 * p + 2) * w])
            o_ref[:, j * LANES:(j + 1) * LANES] = o2.astype(BF16)


def _attention(q, k, v, sinks, batch, seq):
    n, dq = q.shape
    dkv = k.shape[1]
    nb = seq // WINDOW
    cur = lambda b, i: (b * nb + i, 0)
    prev = lambda b, i: (b * nb + jnp.maximum(i - 1, 0), 0)
    return pl.pallas_call(
        _attn_kernel,
        grid=(batch, nb),
        in_specs=[
            pl.BlockSpec(memory_space=pltpu.SMEM),
            pl.BlockSpec((WINDOW, dq), cur),
            pl.BlockSpec((WINDOW, dkv), prev), pl.BlockSpec((WINDOW, dkv), cur),
            pl.BlockSpec((WINDOW, dkv), prev), pl.BlockSpec((WINDOW, dkv), cur),
        ],
        out_specs=pl.BlockSpec((WINDOW, dq), cur),
        out_shape=jax.ShapeDtypeStruct((n, dq), BF16),
        compiler_params=_cparams(("parallel", "parallel")),
        name="swa_attention",
    )(sinks, q, k, k, v, v)


def _proj_kernel(a_ref, w_ref, res_ref, g_ref, h_ref, hn_ref, *, tiled_rows):
    if len(a_ref.shape) == 3:
        a = jnp.concatenate([a_ref[j] for j in range(a_ref.shape[0])], axis=-1)
    else:
        a = a_ref[...]
    h = res_ref[...] + jnp.dot(a, w_ref[...], preferred_element_type=F32)
    h_ref[...] = h
    hn = _rms(h, g_ref[...])
    if tiled_rows:
        tm = h.shape[0]
        for s in range(h.shape[1] // LANES):
            hn_ref[pl.ds(s, tm, stride=SUBLANES), :] = hn[:, s * LANES:(s + 1) * LANES]
    else:
        hn_ref[...] = hn.astype(hn_ref.dtype)


def _proj_res_norm(a, w, res, g, *, tiled_rows):
    kdim, d = w.shape
    n = res.shape[0]
    tm = ROW_TILE
    row = lambda i: (i, 0)
    const = lambda i: (0, 0)
    if a.ndim == 3:
        a_spec = pl.BlockSpec((a.shape[0], tm, a.shape[2]), lambda i: (0, i, 0))
    else:
        a_spec = pl.BlockSpec((tm, kdim), row)
    if tiled_rows:
        assert d == SUBLANES * LANES
        hn_spec = pl.BlockSpec((tm * SUBLANES, LANES), row)
        hn_shape = jax.ShapeDtypeStruct((n * SUBLANES, LANES), F32)
    else:
        hn_spec = pl.BlockSpec((tm, d), row)
        hn_shape = jax.ShapeDtypeStruct((n, d), BF16)
    return pl.pallas_call(
        functools.partial(_proj_kernel, tiled_rows=tiled_rows),
        grid=(n // tm,),
        in_specs=[a_spec, pl.BlockSpec((kdim, d), const),
                  pl.BlockSpec((tm, d), row), pl.BlockSpec((1, d), const)],
        out_specs=[pl.BlockSpec((tm, d), row), hn_spec],
        out_shape=[jax.ShapeDtypeStruct((n, d), F32), hn_shape],
        compiler_params=_cparams(("parallel",)),
        name="proj_res_norm",
    )(a, w, res, g)


def _swiglu_partial(xb, wg, wu, wd):
    hg = jnp.dot(xb, wg, preferred_element_type=F32)
    hu = jnp.dot(xb, wu, preferred_element_type=F32)
    act = (_silu(hg) * hu).astype(BF16)
    return jnp.dot(act, wd, preferred_element_type=F32)


def _ffn_dense_kernel(x_ref, wg_ref, wu_ref, wd_ref, res_ref, g_ref, h_ref, hn_ref, acc_ref):
    j = pl.program_id(1)
    last = pl.num_programs(1) - 1
    part = _swiglu_partial(x_ref[...], wg_ref[...], wu_ref[...], wd_ref[...])

    @pl.when(j == 0)
    def _():
        acc_ref[...] = res_ref[...] + part

    @pl.when((j > 0) & (j < last))
    def _():
        acc_ref[...] += part

    @pl.when(j == last)
    def _():
        h = acc_ref[...] + part
        h_ref[...] = h
        hn_ref[...] = _rms(h, g_ref[...]).astype(hn_ref.dtype)


def _ffn_dense(xn, wg, wu, wd, res, g):
    n, d = xn.shape
    ff = wg.shape[1]
    tm, tf = ROW_TILE, FF_TILE
    assert ff % tf == 0 and ff // tf >= 2
    row = lambda i, j: (i, 0)
    return pl.pallas_call(
        _ffn_dense_kernel,
        grid=(n // tm, ff // tf),
        in_specs=[
            pl.BlockSpec((tm, d), row),
            pl.BlockSpec((d, tf), lambda i, j: (0, j)),
            pl.BlockSpec((d, tf), lambda i, j: (0, j)),
            pl.BlockSpec((tf, d), lambda i, j: (j, 0)),
            pl.BlockSpec((tm, d), row),
            pl.BlockSpec((1, d), lambda i, j: (0, 0)),
        ],
        out_specs=[pl.BlockSpec((tm, d), row), pl.BlockSpec((tm, d), row)],
        out_shape=[jax.ShapeDtypeStruct((n, d), F32), jax.ShapeDtypeStruct((n, d), BF16)],
        scratch_shapes=[pltpu.VMEM((tm, d), F32)],
        compiler_params=_cparams(("parallel", "arbitrary")),
        name="swiglu_dense",
    )(xn, wg, wu, wd, res, g)


def _ffn_moe_kernel(vt_ref, ve_ref, vlo_ref, vhi_ref, vfirst_ref, nv_ref,
                    x_ref, wg_ref, wu_ref, wd_ref, y_ref, acc_ref):
    v = pl.program_id(0)
    j = pl.program_id(1)
    last = pl.num_programs(1) - 1
    tm, d = acc_ref.shape
    nslab = d // LANES

    @pl.when(v < nv_ref[0])
    def _():
        slabs = [x_ref[pl.ds(s, tm, stride=SUBLANES), :] for s in range(nslab)]
        xb = jnp.concatenate(slabs, axis=-1).astype(BF16)
        part = _swiglu_partial(xb, wg_ref[...].astype(BF16), wu_ref[...].astype(BF16), wd_ref[...].astype(BF16))

        @pl.when(j == 0)
        def _():
            acc_ref[...] = part

        @pl.when((j > 0) & (j < last))
        def _():
            acc_ref[...] += part

        @pl.when(j == last)
        def _():
            r = vt_ref[v] * tm + lax.broadcasted_iota(I32, (tm, 1), 0)
            mine = (r >= vlo_ref[v]) & (r < vhi_ref[v])

            @pl.when(vfirst_ref[v] == 1)
            def _():
                for s in range(nslab):
                    cs = slice(s * LANES, (s + 1) * LANES)
                    new = acc_ref[:, cs] + part[:, cs]
                    y_ref[pl.ds(s, tm, stride=SUBLANES), :] = jnp.where(mine, new, 0.0)

            @pl.when(vfirst_ref[v] != 1)
            def _():
                for s in range(nslab):
                    cs = slice(s * LANES, (s + 1) * LANES)
                    rows = pl.ds(s, tm, stride=SUBLANES)
                    new = acc_ref[:, cs] + part[:, cs]
                    y_ref[rows, :] = jnp.where(mine, new, y_ref[rows, :])


def _ffn_moe(xs_t, wg, wu, wd, visits, n_visits_max):
    vt, ve, vlo, vhi, vfirst, nv = visits
    rows8, _ = xs_t.shape
    p = rows8 // SUBLANES
    d = wg.shape[1]
    ff = wg.shape[2]
    tm, tf = ROW_TILE, FF_TILE
    nf = ff // tf
    assert ff % tf == 0 and nf >= 2

    def jj(v, j, nv_ref):
        return jnp.where(v < nv_ref[0], j, nf - 1)

    xmap = lambda v, j, vt, ve, vlo, vhi, vf, nv: (vt[v], 0)
    gmap = lambda v, j, vt, ve, vlo, vhi, vf, nv: (ve[v], 0, jj(v, j, nv))
    dmap = lambda v, j, vt, ve, vlo, vhi, vf, nv: (ve[v], jj(v, j, nv), 0)
    grid_spec = pltpu.PrefetchScalarGridSpec(
        num_scalar_prefetch=6,
        grid=(n_visits_max, nf),
        in_specs=[
            pl.BlockSpec((tm * SUBLANES, LANES), xmap),
            pl.BlockSpec((None, d, tf), gmap),
            pl.BlockSpec((None, d, tf), gmap),
            pl.BlockSpec((None, tf, d), dmap),
        ],
        out_specs=pl.BlockSpec((tm * SUBLANES, LANES), xmap),
        scratch_shapes=[pltpu.VMEM((tm, d), F32)],
    )
    return pl.pallas_call(
        _ffn_moe_kernel,
        grid_spec=grid_spec,
        out_shape=jax.ShapeDtypeStruct((p * SUBLANES, LANES), F32),
        compiler_params=_cparams(("arbitrary", "arbitrary")),
        name="swiglu_moe",
    )(vt, ve, vlo, vhi, vfirst, nv, xs_t, wg, wu, wd)


def _mm_kernel(x_ref, w_ref, o_ref):
    acc = jnp.dot(x_ref[...], w_ref[...], preferred_element_type=F32)
    if len(o_ref.shape) == 3:
        for j in range(o_ref.shape[0]):
            o_ref[j] = acc[:, j * LANES:(j + 1) * LANES].astype(o_ref.dtype)
    else:
        o_ref[...] = acc.astype(o_ref.dtype)


def _matmul(x, w, tm, tn, out_dtype, *, head_major=False):
    n, kdim = x.shape
    m = w.shape[1]
    if head_major:
        out_spec = pl.BlockSpec((tn // LANES, tm, LANES), lambda c, r: (c, r, 0))
        out_shape = jax.ShapeDtypeStruct((m // LANES, n, LANES), out_dtype)
    else:
        out_spec = pl.BlockSpec((tm, tn), lambda c, r: (r, c))
        out_shape = jax.ShapeDtypeStruct((n, m), out_dtype)
    return pl.pallas_call(
        _mm_kernel,
        grid=(m // tn, n // tm),
        in_specs=[pl.BlockSpec((tm, kdim), lambda c, r: (r, 0)), pl.BlockSpec((kdim, tn), lambda c, r: (0, c))],
        out_specs=out_spec,
        out_shape=out_shape,
        compiler_params=_cparams(("parallel", "parallel")),
        name="matmul",
    )(x, w)


def _gdn_kernel(alog_ref, dtb_ref, q_ref, k_ref, v_ref, z_ref, gt_ref, cwq_ref, cwk_ref, cwv_ref, on_ref,
                o_ref, lo_ref, rhs_ref, u_ref, wq_ref, in_ref, kd_ref, gl_ref):
    hb = pl.program_id(1)
    nh, t, dh = q_ref.shape
    c = GDN_CHUNK
    grp = GDN_GROUP
    gph = t // (c * grp)
    ngroups = nh * gph
    halo = SUBLANES
    ii = lax.broadcasted_iota(I32, (c, c), 0)
    jx = lax.broadcasted_iota(I32, (c, c), 1)
    causal = ii >= jx
    strict = ii > jx
    bits = ii ^ jx
    pos = lax.broadcasted_iota(I32, (c, dh), 0)
    nt = (((1,), (1,)), ((), ()))
    tn = (((0,), (0,)), ((), ()))

    def slot_rows(tk, k, rows_per_chunk=c):
        base = ((tk % GDN_RING) * grp + k) * rows_per_chunk
        return pl.ds(pl.multiple_of(base, rows_per_chunk), rows_per_chunk)

    def conv_silu(x_ref, w_ref, hh, ci, rs, rp):
        x = x_ref[hh, rs, :]
        prev = jnp.where(ci > 0, x_ref[hh, rp, :], 0.0)
        xx = jnp.concatenate([prev, x], axis=0)
        w = w_ref[hh]
        y = x * w[CONV_WIDTH - 1:CONV_WIDTH, :]
        for s in range(1, CONV_WIDTH):
            y = y + pltpu.roll(xx, s, 0)[halo:] * w[CONV_WIDTH - 1 - s:CONV_WIDTH - s, :]
        return _silu(y)

    def l2n(x):
        return x * lax.rsqrt(jnp.sum(x * x, axis=-1, keepdims=True) + EPS)

    def local(tk, stores):
        hh = tk // gph
        head = hb * nh + hh
        neg_a = -jnp.exp(alog_ref[head])
        dtb = dtb_ref[head]
        for k in range(grp):
            ci = (tk % gph) * grp + k
            r0 = pl.multiple_of(ci * c, c)
            rs = pl.ds(r0, c)
            rp = pl.ds(pl.multiple_of(jnp.maximum(r0 - halo, 0), halo), halo)
            qc = l2n(conv_silu(q_ref, cwq_ref, hh, ci, rs, rp)) * (dh ** -0.5)
            kc = l2n(conv_silu(k_ref, cwk_ref, hh, ci, rs, rp))
            vc = conv_silu(v_ref, cwv_ref, hh, ci, rs, rp)
            gt = gt_ref[rs, :]
            lane = lax.broadcasted_iota(I32, gt.shape, 1)
            b_log = jnp.sum(jnp.where(lane == head, gt, 0.0), axis=-1, keepdims=True)
            a_log = jnp.sum(jnp.where(lane == head + B_HEADS, gt, 0.0), axis=-1, keepdims=True)
            beta = jnp.broadcast_to(1.0 / (1.0 + jnp.exp(-b_log)), (c, dh))
            xa = a_log + dtb
            softplus = jnp.maximum(xa, 0.0) + jnp.log(1.0 + jnp.exp(-jnp.abs(xa)))
            gcb = jnp.broadcast_to(neg_a * softplus, (c, dh))
            sh = 1
            while sh < c:
                gcb = gcb + jnp.where(pos >= sh, pltpu.roll(gcb, sh, 0), 0.0)
                sh *= 2
            diff = gcb - gcb.T
            decay = jnp.exp(jnp.where(causal, diff, NEG))
            kb = kc * beta
            both = lax.dot_general(jnp.concatenate([kb.astype(BF16), qc.astype(BF16)], axis=0),
                                   kc.astype(BF16), nt, preferred_element_type=F32)
            eg = jnp.exp(gcb)
            glast = gcb[c - 1:c, :]
            stores.append((lo_ref, slot_rows(tk, k), jnp.where(strict, both[:c] * decay, 0.0).astype(BF16)))
            stores.append((rhs_ref, slot_rows(tk, k), jnp.concatenate([vc * beta, kb * eg], axis=-1)))
            stores.append((wq_ref, pl.ds(pl.multiple_of((((tk % GDN_RING) * grp + k) * 2 + 1) * c, c), c),
                           (qc * eg).astype(BF16)))
            stores.append((in_ref, slot_rows(tk, k), jnp.where(causal, both[c:] * decay, 0.0).astype(BF16)))
            stores.append((kd_ref, slot_rows(tk, k), (kc * jnp.exp(glast - gcb)).astype(BF16)))
            stores.append((gl_ref, slot_rows(tk, k, halo), jnp.broadcast_to(jnp.exp(glast), (halo, dh))))
            yield

    def solve(tk, stores):
        def mask(cond):
            return jnp.where(cond, 1.0, 0.0).astype(BF16)

        eye_b = mask(ii == jx)
        lowers = [lo_ref[slot_rows(tk, k), :] for k in range(grp)]
        base = mask(bits < 2)
        tbs = [eye_b - lower * base for lower in lowers]
        yield
        m = 2
        while m < c:
            level = mask((bits >= m) & (bits < 2 * m))
            ys = [jnp.dot(lower * level, tb, preferred_element_type=F32).astype(BF16)
                  for lower, tb in zip(lowers, tbs)]
            yield
            tbs = [tb - jnp.dot(tb, y, preferred_element_type=F32).astype(BF16) for tb, y in zip(tbs, ys)]
            yield
            m *= 2
        for k, tb in enumerate(tbs):
            rhs = rhs_ref[slot_rows(tk, k), :]
            x = rhs + jnp.dot(tb - eye_b, rhs.astype(BF16), preferred_element_type=F32)
            stores.append((u_ref, slot_rows(tk, k), x[:, :dh]))
            stores.append((wq_ref, pl.ds(pl.multiple_of(((tk % GDN_RING) * grp + k) * 2 * c, c), c),
                           x[:, dh:].astype(BF16)))
        yield

    def scan(tk, k, state):
        hh = tk // gph
        ci = (tk % gph) * grp + k
        rs = pl.ds(pl.multiple_of(ci * c, c), c)
        if k == 0:
            state = jnp.where(tk % gph == 0, 0.0, state)
        wq = wq_ref[pl.ds(pl.multiple_of(((tk % GDN_RING) * grp + k) * 2 * c, 2 * c), 2 * c), :]
        xs = jnp.dot(wq, state.astype(BF16), preferred_element_type=F32)
        vnew = (u_ref[slot_rows(tk, k), :] - xs[:c]).astype(BF16)
        o = xs[c:] + jnp.dot(in_ref[slot_rows(tk, k), :], vnew, preferred_element_type=F32)
        dec = gl_ref[pl.ds(pl.multiple_of(((tk % GDN_RING) * grp + k) * halo, halo), 1), :]
        state = state * dec + lax.dot_general(kd_ref[slot_rows(tk, k), :], vnew, tn, preferred_element_type=F32)
        on = _rms(o, on_ref[...]) * _silu(z_ref[hh, rs, :])
        o_ref[hh, rs, :] = on.astype(o_ref.dtype)
        return state

    n_solve_stages = 2 * (c.bit_length() - 2) + 2

    def tick(tk, state):
        static = isinstance(tk, int)
        do_local = (not static) or tk < ngroups
        do_solve = (not static) or 1 <= tk <= ngroups
        do_scan = (not static) or 2 <= tk <= ngroups + 1
        tk = jnp.int32(tk)
        stores = []
        solver = solve(tk - 1, stores) if do_solve else iter(())
        locals_ = local(tk, stores) if do_local else iter(())
        n_local = n_scan = 0
        for i in range(n_solve_stages):
            next(solver, None)
            while n_local < -((i + 1) * grp // -n_solve_stages):
                next(locals_, None)
                n_local += 1
            while do_scan and n_scan < (i + 1) * grp // n_solve_stages:
                state = scan(tk - 2, n_scan, state)
                n_scan += 1
        for ref, rows, val in stores:
            ref[rows, :] = val
        return state

    state = jnp.zeros((dh, dh), F32)
    state = tick(0, state)
    state = tick(1, state)
    state = lax.fori_loop(2, ngroups, tick, state)
    state = tick(ngroups, state)
    tick(ngroups + 1, state)


def _gated_deltanet(qkvz, gates, conv_w, a_log, dt_bias, o_norm, batch, seq):
    nheads4, n, dh = qkvz.shape
    nh = B_HEADS
    hbk = GDN_HEADS
    nhb = nh // hbk
    c = GDN_CHUNK
    assert nheads4 == 4 * nh and seq % (c * GDN_GROUP) == 0 and nh % hbk == 0
    blk = lambda off: pl.BlockSpec((hbk, seq, dh), lambda b, hb, off=off: (off * nhb + hb, b, 0))
    cw = lambda off: pl.BlockSpec((hbk, CONV_WIDTH, dh), lambda b, hb, off=off: (off * nhb + hb, 0, 0))
    smem = pl.BlockSpec(memory_space=pltpu.SMEM)
    slots = GDN_RING * GDN_GROUP
    scratch = [pltpu.VMEM((slots * c, c), BF16), pltpu.VMEM((slots * c, 2 * dh), F32),
               pltpu.VMEM((slots * c, dh), F32), pltpu.VMEM((slots * 2 * c, dh), BF16),
               pltpu.VMEM((slots * c, c), BF16), pltpu.VMEM((slots * c, dh), BF16),
               pltpu.VMEM((slots * SUBLANES, dh), F32)]
    return pl.pallas_call(
        _gdn_kernel,
        grid=(batch, nhb),
        in_specs=[smem, smem, blk(0), blk(1), blk(2), blk(3),
                  pl.BlockSpec((seq, LANES), lambda b, hb: (b, 0)),
                  cw(0), cw(1), cw(2),
                  pl.BlockSpec((1, dh), lambda b, hb: (0, 0))],
        out_specs=pl.BlockSpec((hbk, seq, dh), lambda b, hb: (hb, b, 0)),
        out_shape=jax.ShapeDtypeStruct((nh, n, dh), BF16),
        scratch_shapes=scratch,
        compiler_params=_cparams(("parallel", "parallel")),
        name="gated_deltanet",
    )(a_log, dt_bias, qkvz, qkvz, qkvz, qkvz, gates, conv_w, conv_w, conv_w, o_norm)


def _route_kernel(hn_ref, wr_ref, e1_ref, e2_ref, w1_ref, w2_ref, r1_ref, r2_ref, cnt_ref, base_ref):
    i = pl.program_id(0)
    tm, d = hn_ref.shape[0] // SUBLANES, SUBLANES * LANES
    ne = wr_ref.shape[0]

    @pl.when(i == 0)
    def _():
        base_ref[...] = jnp.zeros_like(base_ref)

    lg = jnp.zeros((ne, tm), F32)
    for s in range(SUBLANES):
        lg = lg + lax.dot_general(wr_ref[:, s * LANES:(s + 1) * LANES], hn_ref[pl.ds(s, tm, stride=SUBLANES), :],
                                  (((1,), (1,)), ((), ())), precision=lax.Precision.HIGHEST,
                                  preferred_element_type=F32)
    eio = lax.broadcasted_iota(I32, (ne, tm), 0)
    m1 = jnp.max(lg, axis=0, keepdims=True)
    i1 = jnp.min(jnp.where(lg == m1, eio, ne), axis=0, keepdims=True)
    lg2 = jnp.where(eio == i1, -jnp.inf, lg)
    m2 = jnp.max(lg2, axis=0, keepdims=True)
    i2 = jnp.min(jnp.where(lg2 == m2, eio, ne), axis=0, keepdims=True)
    ex = jnp.exp(m2 - m1)
    w1 = 1.0 / (1.0 + ex)
    sel = jnp.where((eio == i1) | (eio == i2), 1.0, 0.0)
    tri = jnp.where(lax.broadcasted_iota(I32, (tm, tm), 0) < lax.broadcasted_iota(I32, (tm, tm), 1), 1.0, 0.0)
    pre = jnp.dot(sel.astype(BF16), tri.astype(BF16), preferred_element_type=F32) + base_ref[:, 0:1]
    r1 = jnp.sum(jnp.where(eio == i1, pre, 0.0), axis=0, keepdims=True)
    r2 = jnp.sum(jnp.where(eio == i2, pre, 0.0), axis=0, keepdims=True)
    base_ref[...] += jnp.sum(sel, axis=1, keepdims=True)
    e1_ref[0] = i1
    e2_ref[0] = i2
    w1_ref[0] = w1
    w2_ref[0] = ex * w1
    r1_ref[0] = r1.astype(I32)
    r2_ref[0] = r2.astype(I32)
    cnt_ref[...] = base_ref[...]


def _route(hn_t, wr_t):
    n = hn_t.shape[0] // SUBLANES
    ne = wr_t.shape[0]
    tm = ROW_TILE
    nt = n // tm
    vec = pl.BlockSpec((1, 1, tm), lambda i: (i, 0, 0))
    ish = jax.ShapeDtypeStruct((nt, 1, tm), I32)
    fsh = jax.ShapeDtypeStruct((nt, 1, tm), F32)
    return pl.pallas_call(
        _route_kernel,
        grid=(nt,),
        in_specs=[pl.BlockSpec((tm * SUBLANES, LANES), lambda i: (i, 0)),
                  pl.BlockSpec((ne, SUBLANES * LANES), lambda i: (0, 0))],
        out_specs=[vec, vec, vec, vec, vec, vec, pl.BlockSpec((ne, LANES), lambda i: (0, 0))],
        out_shape=[ish, ish, fsh, fsh, ish, ish, jax.ShapeDtypeStruct((ne, LANES), F32)],
        scratch_shapes=[pltpu.VMEM((ne, LANES), F32)],
        compiler_params=_cparams(("arbitrary",)),
        name="moe_route",
    )(hn_t, wr_t)


def _dispatch_kernel(p1_ref, p2_ref, hn_ref, xs_ref, sem):
    i = pl.program_id(0)
    tg = hn_ref.shape[0] // SUBLANES

    def tile_copy(k, pos):
        src = hn_ref.at[pl.ds(pl.multiple_of(k * SUBLANES, SUBLANES), SUBLANES), :]
        dst = xs_ref.at[pl.ds(pl.multiple_of(pos * SUBLANES, SUBLANES), SUBLANES), :]
        return pltpu.make_async_copy(src, dst, sem)

    def issue(k, carry):
        t = i * tg + k
        tile_copy(k, p1_ref[t]).start(priority=0)
        tile_copy(k, p2_ref[t]).start(priority=1)
        return carry

    lax.fori_loop(0, tg, issue, 0, unroll=8)

    whole = pltpu.make_async_copy(hn_ref, xs_ref.at[pl.ds(0, tg * SUBLANES), :], sem)
    whole.wait()
    whole.wait()


def _dispatch(pos1, pos2, hn_t):
    rows8 = hn_t.shape[0]
    n = rows8 // SUBLANES
    tg = ROW_TILE
    grid_spec = pltpu.PrefetchScalarGridSpec(
        num_scalar_prefetch=2,
        grid=(n // tg,),
        in_specs=[pl.BlockSpec((tg * SUBLANES, LANES), lambda i, p1, p2: (i, 0))],
        out_specs=pl.BlockSpec(memory_space=pl.ANY),
        scratch_shapes=[pltpu.SemaphoreType.DMA(())],
    )
    return pl.pallas_call(
        _dispatch_kernel,
        grid_spec=grid_spec,
        out_shape=jax.ShapeDtypeStruct((2 * rows8, LANES), F32),
        compiler_params=_cparams(("arbitrary",)),
        name="moe_dispatch",
    )(pos1, pos2, hn_t)


def _combine_kernel(p1_ref, p2_ref, y_ref, h_ref, w1_ref, w2_ref, g_ref, o_ref, y1_ref, y2_ref, sem):
    i = pl.program_id(0)
    nsteps = pl.num_programs(0)
    tc, d = h_ref.shape
    nslab = d // LANES
    slot = i % 2

    def tile_copy(pos, buf, sl, k):
        src = y_ref.at[pl.ds(pl.multiple_of(pos * SUBLANES, SUBLANES), SUBLANES), :]
        dst = buf.at[sl, pl.ds(pl.multiple_of(k * SUBLANES, SUBLANES), SUBLANES), :]
        return pltpu.make_async_copy(src, dst, sem.at[sl])

    def start_gather(step, sl, k):
        t = step * tc + k
        tile_copy(p1_ref[t], y1_ref, sl, k).start(priority=0)
        tile_copy(p2_ref[t], y2_ref, sl, k).start(priority=1)

    def wait_slot(sl):
        whole = pl.ds(0, tc * SUBLANES)
        pltpu.make_async_copy(y_ref.at[whole, :], y1_ref.at[sl], sem.at[sl]).wait()
        pltpu.make_async_copy(y_ref.at[whole, :], y2_ref.at[sl], sem.at[sl]).wait()

    @pl.when(i == 0)
    def _():
        def issue(k, carry):
            start_gather(0, 0, k)
            return carry

        lax.fori_loop(0, tc, issue, 0, unroll=8)

    wait_slot(slot)

    nxt = jnp.minimum(i + 1, nsteps - 1)
    per_slab = tc // nslab
    w1 = w1_ref[...]
    w2 = w2_ref[...]
    slabs = []
    ss = jnp.zeros((tc, 1), F32)
    for s in range(nslab):
        for k in range(s * per_slab, (s + 1) * per_slab):
            start_gather(nxt, 1 - slot, k)
        rows = pl.ds(s, tc, stride=SUBLANES)
        hs = h_ref[:, s * LANES:(s + 1) * LANES] + w1 * y1_ref[slot, rows, :] + w2 * y2_ref[slot, rows, :]
        ss = ss + jnp.sum(hs * hs, axis=-1, keepdims=True)
        slabs.append(hs)
    inv = lax.rsqrt(ss / d + EPS)
    for s, hs in enumerate(slabs):
        o_ref[:, s * LANES:(s + 1) * LANES] = hs * inv * g_ref[:, s * LANES:(s + 1) * LANES]

    @pl.when(i == nsteps - 1)
    def _():
        wait_slot(1 - slot)


def _combine(pos1, pos2, y_t, h, w1, w2, g):
    n, d = h.shape
    tc = COMBINE_TILE
    row = lambda i, p1, p2: (i, 0)
    grid_spec = pltpu.PrefetchScalarGridSpec(
        num_scalar_prefetch=2,
        grid=(n // tc,),
        in_specs=[pl.BlockSpec(memory_space=pl.ANY),
                  pl.BlockSpec((tc, d), row),
                  pl.BlockSpec((tc, 1), row), pl.BlockSpec((tc, 1), row),
                  pl.BlockSpec((1, d), lambda i, p1, p2: (0, 0))],
        out_specs=pl.BlockSpec((tc, d), row),
        scratch_shapes=[pltpu.VMEM((2, tc * SUBLANES, LANES), F32), pltpu.VMEM((2, tc * SUBLANES, LANES), F32),
                        pltpu.SemaphoreType.DMA((2,))],
    )
    return pl.pallas_call(
        _combine_kernel,
        grid_spec=grid_spec,
        out_shape=jax.ShapeDtypeStruct((n, d), F32),
        compiler_params=_cparams(("arbitrary",)),
        name="moe_combine",
    )(pos1, pos2, y_t, h, w1, w2, g)


def _visit_plan(counts, n_slots):
    ne = counts.shape[0]
    tm = ROW_TILE
    n_tiles = n_slots // tm
    n_visits_max = n_tiles + ne - 1
    ends = jnp.cumsum(counts)
    starts = ends - counts
    t0 = jnp.arange(n_tiles, dtype=I32)[:, None] * tm
    lo = jnp.maximum(starts[None, :], t0)
    hi = jnp.minimum(ends[None, :], t0 + tm)
    hit = (hi > lo).reshape(-1)
    nv = jnp.sum(hit).astype(I32)
    idx = jnp.nonzero(hit, size=n_visits_max, fill_value=0)[0].astype(I32)
    last = idx[jnp.maximum(nv - 1, 0)]
    idx = jnp.where(jnp.arange(n_visits_max) < nv, idx, last)
    vt = idx // ne
    ve = idx % ne
    vlo = starts[ve]
    vhi = ends[ve]
    vfirst = jnp.concatenate([jnp.ones((1,), I32), (vt[1:] != vt[:-1]).astype(I32)])
    return (vt, ve, vlo.astype(I32), vhi.astype(I32), vfirst, nv.reshape(1)), n_visits_max


def _rope_tables(seq):
    half = A_HEAD_DIM // 2
    inv = 1.0 / (ROPE_THETA ** (jnp.arange(0, A_HEAD_DIM, 2, dtype=F32) / A_HEAD_DIM))
    ang = jnp.arange(seq, dtype=F32)[:, None] * inv[None, :]
    cos, sin = jnp.cos(ang), jnp.sin(ang)
    reps = LANES // A_HEAD_DIM
    cos_t = jnp.tile(jnp.concatenate([cos, cos], axis=-1), (1, reps))
    sin_t = jnp.tile(jnp.concatenate([-sin, sin], axis=-1), (1, reps))
    return cos_t, sin_t


def kernel(x, a_norm, a_w_qkv, a_b_qkv, a_sinks, a_w_o, f_norm, f_w_gate, f_w_up, f_w_down,
           b_norm, b_w_in, b_conv_w, b_a_log, b_dt_bias, b_o_norm, b_w_o,
           m_norm, m_w_router, m_w_gate, m_w_up, m_w_down, final_norm):
    batch, seq, d = x.shape
    n = batch * seq
    x2 = x.reshape(n, d)
    row = lambda v: v.reshape(1, -1).astype(F32)

    nq = A_HEADS * A_HEAD_DIM
    nkv = A_KV_HEADS * A_HEAD_DIM
    dup = jnp.repeat(jnp.arange(A_KV_HEADS), 2)[:, None] * A_HEAD_DIM + jnp.arange(A_HEAD_DIM)[None, :]
    cols = jnp.concatenate([jnp.arange(nq), nq + dup.reshape(-1), nq + nkv + dup.reshape(-1)])
    w_qkv = a_w_qkv[0][:, cols].astype(BF16)
    b_qkv = a_b_qkv[0][cols].reshape(1, -1)
    cos_t, sin_t = _rope_tables(seq)
    scale = A_HEAD_DIM ** -0.5
    q, k, v = _qkv_proj(x2, row(a_norm[0]), w_qkv, b_qkv, cos_t * scale, sin_t * scale, cos_t, sin_t, seq)
    att = _attention(q, k, v, a_sinks[0].astype(F32), batch, seq)
    h, hn = _proj_res_norm(att, a_w_o[0].astype(BF16), x2, row(f_norm[0]), tiled_rows=False)
    h, hn = _ffn_dense(hn, f_w_gate[0].astype(BF16), f_w_up[0].astype(BF16), f_w_down[0].astype(BF16),
                       h, row(b_norm[0]))

    nqkvz = 4 * B_HEADS * B_HEAD_DIM
    w_in = b_w_in[0]
    qkvz = _matmul(hn, w_in[:, :nqkvz].astype(BF16), 2 * ROW_TILE, B_HEADS * B_HEAD_DIM, F32, head_major=True)
    w_gates = jnp.pad(w_in[:, nqkvz:], ((0, 0), (0, LANES - 2 * B_HEADS))).astype(BF16)
    gates = _matmul(hn, w_gates, 2 * ROW_TILE, LANES, F32)
    conv_w = b_conv_w[0].astype(F32).reshape(CONV_WIDTH, 3 * B_HEADS, B_HEAD_DIM).transpose(1, 0, 2)
    gdn = _gated_deltanet(qkvz, gates, conv_w, b_a_log[0].astype(F32), b_dt_bias[0].astype(F32),
                          row(b_o_norm[0]), batch, seq)
    h, hn_t = _proj_res_norm(gdn, b_w_o[0].astype(BF16), h, row(m_norm[0]), tiled_rows=True)

    e1, e2, w1, w2, r1, r2, cnt = _route(hn_t, m_w_router[0].T.astype(F32))
    counts = cnt[:, 0].astype(I32)
    starts = jnp.cumsum(counts) - counts
    e1, e2, r1, r2 = (a.reshape(n) for a in (e1, e2, r1, r2))
    pos1 = starts[e1] + r1
    pos2 = starts[e2] + r2
    visits, n_visits_max = _visit_plan(counts, 2 * n)
    xs_t = _dispatch(pos1, pos2, hn_t)
    y_t = _ffn_moe(xs_t, m_w_gate[0], m_w_up[0].astype(BF16), m_w_down[0], visits, n_visits_max)
    out = _combine(pos1, pos2, y_t, h, w1.reshape(n, 1), w2.reshape(n, 1), row(final_norm))
    return out.reshape(batch, seq, d)
```
